```python
import math
import jax, jax.numpy as jnp
from jax import lax
import numpy as np

D_MODEL = 1024
BATCH = 4
SEQ = 4096
DEPTH = 4

N_MIXERS = 3
N_A = (DEPTH + 2) // 3
N_B = (DEPTH + 1) // 3
N_C = DEPTH // 3

ALPHA = (2.0 * DEPTH) ** 0.25
BETA = (8.0 * DEPTH) ** -0.25
LN_EPS = 1e-5

GLA_HEADS = 4
GLA_DK = D_MODEL // 2
GLA_DV = D_MODEL
GLA_HK = GLA_DK // GLA_HEADS
GLA_HV = GLA_DV // GLA_HEADS
GLA_RANK = 16
GLA_TAU = 16.0
GLA_CHUNK = 32
GLA_IN = 2 * GLA_DK + 2 * GLA_DV + GLA_RANK

RG_BW = 128
RG_WIDTH = ((4 * D_MODEL // 3) // RG_BW) * RG_BW
RG_BLOCKS = RG_WIDTH // RG_BW
RG_CONV = 4
RG_C = 8.0

FOX_HD = 64
FOX_HEADS = D_MODEL // FOX_HD
FOX_BLOCK = 128
FOX_IN = 4 * D_MODEL + FOX_HEADS

N_EXPERTS = 32
TOP_K = 4
D_FF = D_MODEL
SWIGLU_LIMIT = 7.0
SWIGLU_ALPHA = 1.702
MOE_BLOCK = 128

kernel_name = "hybrid_gla_rglru_fox_moe_deepnorm_adaln"


def layer_norm(x, g, b):
    xf = x.astype(jnp.float32)
    mu = jnp.mean(xf, axis=-1, keepdims=True)
    var = jnp.mean(jnp.square(xf - mu), axis=-1, keepdims=True)
    return ((xf - mu) * lax.rsqrt(var + LN_EPS) * g + b).astype(x.dtype)


def rms_norm(x, g):
    xf = x.astype(jnp.float32)
    return (xf * lax.rsqrt(jnp.mean(jnp.square(xf), axis=-1, keepdims=True) + LN_EPS) * g).astype(x.dtype)


def gla_mixer(h, w_in, w_gate_up, b_gate, norm_g, w_out):
    B, S, _ = h.shape
    H, C = GLA_HEADS, GLA_CHUNK
    NC = S // C
    proj = h @ w_in
    q, k, v, r, g_lr = jnp.split(proj, [GLA_DK, 2 * GLA_DK, 2 * GLA_DK + GLA_DV, 2 * GLA_DK + 2 * GLA_DV], axis=-1)
    log_a = jax.nn.log_sigmoid((g_lr @ w_gate_up + b_gate).astype(jnp.float32)) / GLA_TAU

    def chunked(t, dh):
        return t.reshape(B, NC, C, H, dh).transpose(0, 3, 1, 2, 4)

    qc = chunked(q, GLA_HK).astype(jnp.float32) * (GLA_HK ** -0.5)
    kc = chunked(k, GLA_HK).astype(jnp.float32)
    vc = chunked(v, GLA_HV).astype(jnp.float32)
    bcum = jnp.cumsum(chunked(log_a, GLA_HK), axis=3)
    blast = bcum[:, :, :, -1:, :]
    q_dec = qc * jnp.exp(bcum)
    k_inv = kc * jnp.exp(-bcum)
    k_end = kc * jnp.exp(blast - bcum)
    causal = jnp.tril(jnp.ones((C, C), dtype=bool))
    scores = jnp.where(causal, jnp.einsum('bhnid,bhnjd->bhnij', q_dec, k_inv), 0.0)
    o_intra = jnp.einsum('bhnij,bhnje->bhnie', scores, vc)

    def step(state, inp):
        q_n, k_n, v_n, dec_n = inp
        o_n = jnp.einsum('bhid,bhde->bhie', q_n, state)
        state = dec_n[..., None] * state + jnp.einsum('bhjd,bhje->bhde', k_n, v_n)
        return state, o_n

    xs = (jnp.moveaxis(q_dec, 2, 0), jnp.moveaxis(k_end, 2, 0), jnp.moveaxis(vc, 2, 0),
          jnp.moveaxis(jnp.exp(blast[:, :, :, 0, :]), 2, 0))
    state0 = jnp.zeros((B, H, GLA_HK, GLA_HV), jnp.float32)
    _, o_inter = lax.scan(step, state0, xs)
    o = o_intra + jnp.moveaxis(o_inter, 0, 2)
    o = o.transpose(0, 2, 3, 1, 4).reshape(B, S, H, GLA_HV)
    o = (o * lax.rsqrt(jnp.mean(jnp.square(o), axis=-1, keepdims=True) + LN_EPS)).reshape(B, S, GLA_DV) * norm_g
    o = o.astype(h.dtype) * jax.nn.silu(r)
    return o @ w_out


def rglru_mixer(h, w_in, conv_w, conv_b, w_rg, b_rg, w_ig, b_ig, lam, w_out):
    B, S, _ = h.shape
    proj = h @ w_in
    y_br, x_br = jnp.split(proj, 2, axis=-1)
    y_br = jax.nn.gelu(y_br, approximate=True)
    xp = jnp.pad(x_br, ((0, 0), (RG_CONV - 1, 0), (0, 0)))
    xc = sum(xp[:, j:j + S] * conv_w[j] for j in range(RG_CONV)) + conv_b
    xblk = xc.reshape(B, S, RG_BLOCKS, RG_BW)
    r = jax.nn.sigmoid(jnp.einsum('bsnj,njk->bsnk', xblk, w_rg).reshape(B, S, RG_WIDTH) + b_rg)
    i = jax.nn.sigmoid(jnp.einsum('bsnj,njk->bsnk', xblk, w_ig).reshape(B, S, RG_WIDTH) + b_ig)
    log_a = -RG_C * r.astype(jnp.float32) * jax.nn.softplus(-lam.astype(jnp.float32))
    a = jnp.exp(log_a)
    u = jnp.sqrt(-jnp.expm1(2.0 * log_a)) * (i * xc).astype(jnp.float32)

    def combine(e1, e2):
        a1, b1 = e1
        a2, b2 = e2
        return a1 * a2, a2 * b1 + b2

    _, hs = lax.associative_scan(combine, (a, u), axis=1)
    return (hs.astype(h.dtype) * y_br) @ w_out


def fox_mixer(h, w_in, b_f, q_norm_g, k_norm_g, w_out):
    B, S, D = h.shape
    H, hd = FOX_HEADS, FOX_HD
    proj = h @ w_in
    q, k, v, f_logit, og = jnp.split(proj, [D, 2 * D, 3 * D, 3 * D + H], axis=-1)
    q = rms_norm(q.reshape(B, S, H, hd), q_norm_g).transpose(0, 2, 1, 3)
    k = rms_norm(k.reshape(B, S, H, hd), k_norm_g).transpose(0, 2, 1, 3)
    v = v.reshape(B, S, H, hd).transpose(0, 2, 1, 3)
    log_f = jax.nn.log_sigmoid((f_logit + b_f).astype(jnp.float32))
    dcum = jnp.cumsum(log_f, axis=1).transpose(0, 2, 1)
    scale = hd ** -0.5
    outs = []
    for blk in range(S // FOX_BLOCK):
        q0, q1 = blk * FOX_BLOCK, (blk + 1) * FOX_BLOCK
        logits = (jnp.einsum('bhqd,bhkd->bhqk', q[:, :, q0:q1], k[:, :, :q1]).astype(jnp.float32) * scale
                  + dcum[:, :, q0:q1, None] - dcum[:, :, None, :q1])
        mask = jnp.arange(q1)[None, :] <= jnp.arange(q0, q1)[:, None]
        p = jax.nn.softmax(jnp.where(mask, logits, -jnp.inf), axis=-1)
        outs.append(jnp.einsum('bhqk,bhkd->bhqd', p.astype(v.dtype), v[:, :, :q1]))
    o = jnp.concatenate(outs, axis=2).transpose(0, 2, 1, 3).reshape(B, S, D)
    return (o * jax.nn.sigmoid(og)) @ w_out


def moe_ffn(h, w_router, b_router, w_gu, b_gu, w_down, b_down):
    B, S, D = h.shape
    T = B * S
    xt = h.reshape(T, D)
    logits = (xt @ w_router + b_router).astype(jnp.float32)
    top_logit, top_idx = lax.top_k(logits, TOP_K)
    top_w = jax.nn.softmax(top_logit, axis=-1)
    n_assign = T * TOP_K
    n_pad = -(-(n_assign + N_EXPERTS * (MOE_BLOCK - 1)) // MOE_BLOCK) * MOE_BLOCK
    n_blocks = n_pad // MOE_BLOCK
    flat_e = top_idx.reshape(-1)
    flat_tok = jnp.arange(n_assign, dtype=jnp.int32) // TOP_K
    order = jnp.argsort(flat_e)
    sorted_e = flat_e[order]
    counts = jnp.bincount(flat_e, length=N_EXPERTS)
    starts = jnp.cumsum(counts) - counts
    padded = (counts + MOE_BLOCK - 1) // MOE_BLOCK * MOE_BLOCK
    pad_ends = jnp.cumsum(padded)
    pad_starts = pad_ends - padded
    dest = pad_starts[sorted_e] + jnp.arange(n_assign) - starts[sorted_e]
    tok_buf = jnp.full((n_pad,), T, jnp.int32).at[dest].set(flat_tok[order])
    w_buf = jnp.zeros((n_pad,), jnp.float32).at[dest].set(top_w.reshape(-1)[order])
    block_e = jnp.minimum(jnp.searchsorted(pad_ends, jnp.arange(n_blocks) * MOE_BLOCK, side='right'), N_EXPERTS - 1)
    x_pad = jnp.concatenate([xt, jnp.zeros((1, D), xt.dtype)], axis=0)
    xb = x_pad[tok_buf].reshape(n_blocks, MOE_BLOCK, D)

    def expert_block(args):
        xblk, e = args
        gu = xblk @ w_gu[e] + b_gu[e]
        gate = jnp.minimum(gu[:, 0::2], SWIGLU_LIMIT)
        up = jnp.clip(gu[:, 1::2], -SWIGLU_LIMIT, SWIGLU_LIMIT)
        glu = gate * jax.nn.sigmoid(SWIGLU_ALPHA * gate)
        return ((up + 1.0) * glu) @ w_down[e] + b_down[e]

    yb = lax.map(expert_block, (xb, block_e)).reshape(n_pad, D)
    out = jnp.zeros((T + 1, D), jnp.float32).at[tok_buf].add(yb.astype(jnp.float32) * w_buf[:, None])[:T]
    return out.astype(h.dtype).reshape(B, S, D)


def setup_inputs(seed: int = 0) -> dict:
    key = jax.random.key(seed)
    ks = iter(jax.random.split(key, 40))

    def nrm(shape, scale):
        return jax.random.normal(next(ks), shape, jnp.float32) * scale

    D = D_MODEL
    inp = {}
    inp['x'] = nrm((BATCH, SEQ, D), 1.0)
    inp['c'] = nrm((BATCH, D), 1.0)
    inp['ada_w'] = nrm((DEPTH, D, 6 * D), 0.1 * D ** -0.5)
    inp['ada_b'] = nrm((DEPTH, 6 * D), 0.02)
    inp['ln_g'] = 1.0 + nrm((DEPTH, 2, D), 0.02)
    inp['ln_b'] = nrm((DEPTH, 2, D), 0.02)
    inp['gla_w_in'] = nrm((N_A, D, GLA_IN), D ** -0.5)
    inp['gla_w_gate_up'] = nrm((N_A, GLA_RANK, GLA_DK), GLA_RANK ** -0.5)
    inp['gla_b_gate'] = nrm((N_A, GLA_DK), 0.5)
    inp['gla_norm_g'] = 1.0 + nrm((N_A, GLA_DV), 0.02)
    inp['gla_w_out'] = nrm((N_A, GLA_DV, D), BETA * GLA_DV ** -0.5)
    inp['rg_w_in'] = nrm((N_B, D, 2 * RG_WIDTH), D ** -0.5)
    inp['rg_conv_w'] = nrm((N_B, RG_CONV, RG_WIDTH), RG_CONV ** -0.5)
    inp['rg_conv_b'] = nrm((N_B, RG_WIDTH), 0.02)
    inp['rg_w_rg'] = nrm((N_B, RG_BLOCKS, RG_BW, RG_BW), RG_BW ** -0.5)
    inp['rg_b_rg'] = nrm((N_B, RG_WIDTH), 0.02)
    inp['rg_w_ig'] = nrm((N_B, RG_BLOCKS, RG_BW, RG_BW), RG_BW ** -0.5)
    inp['rg_b_ig'] = nrm((N_B, RG_WIDTH), 0.02)
    a_c = jax.random.uniform(next(ks), (N_B, RG_WIDTH), jnp.float32, 0.9, 0.999)
    s = a_c ** (1.0 / RG_C)
    inp['rg_lambda'] = jnp.log(s) - jnp.log1p(-s)
    inp['rg_w_out'] = nrm((N_B, RG_WIDTH, D), BETA * RG_WIDTH ** -0.5)
    inp['fox_w_in'] = nrm((N_C, D, FOX_IN), D ** -0.5)
    inp['fox_b_f'] = 2.0 + nrm((N_C, FOX_HEADS), 0.5)
    inp['fox_q_norm_g'] = 1.0 + nrm((N_C, FOX_HD), 0.02)
    inp['fox_k_norm_g'] = 1.0 + nrm((N_C, FOX_HD), 0.02)
    inp['fox_w_out'] = nrm((N_C, D, D), BETA * D ** -0.5)
    inp['moe_w_router'] = nrm((DEPTH, D, N_EXPERTS), D ** -0.5)
    inp['moe_b_router'] = nrm((DEPTH, N_EXPERTS), 0.01)
    inp['moe_w_gu'] = nrm((DEPTH, N_EXPERTS, D, 2 * D_FF), D ** -0.5)
    inp['moe_b_gu'] = nrm((DEPTH, N_EXPERTS, 2 * D_FF), 0.02)
    inp['moe_w_down'] = nrm((DEPTH, N_EXPERTS, D_FF, D), BETA * D_FF ** -0.5)
    inp['moe_b_down'] = nrm((DEPTH, N_EXPERTS, D), 0.02)
    return inp


def reference(x, c, ada_w, ada_b, ln_g, ln_b,
              gla_w_in, gla_w_gate_up, gla_b_gate, gla_norm_g, gla_w_out,
              rg_w_in, rg_conv_w, rg_conv_b, rg_w_rg, rg_b_rg, rg_w_ig, rg_b_ig, rg_lambda, rg_w_out,
              fox_w_in, fox_b_f, fox_q_norm_g, fox_k_norm_g, fox_w_out,
              moe_w_router, moe_b_router, moe_w_gu, moe_b_gu, moe_w_down, moe_b_down):
    c_act = jax.nn.silu(c)
    for l in range(DEPTH):
        mod = c_act @ ada_w[l] + ada_b[l]
        sh1, sc1, g1, sh2, sc2, g2 = [m[:, None, :] for m in jnp.split(mod, 6, axis=-1)]
        h = x * (1.0 + sc1) + sh1
        kind, j = l % N_MIXERS, l // N_MIXERS
        if kind == 0:
            y = gla_mixer(h, gla_w_in[j], gla_w_gate_up[j], gla_b_gate[j], gla_norm_g[j], gla_w_out[j])
        elif kind == 1:
            y = rglru_mixer(h, rg_w_in[j], rg_conv_w[j], rg_conv_b[j], rg_w_rg[j], rg_b_rg[j],
                            rg_w_ig[j], rg_b_ig[j], rg_lambda[j], rg_w_out[j])
        else:
            y = fox_mixer(h, fox_w_in[j], fox_b_f[j], fox_q_norm_g[j], fox_k_norm_g[j], fox_w_out[j])
        x = layer_norm(ALPHA * x + (1.0 + g1) * y, ln_g[l, 0], ln_b[l, 0])
        h = x * (1.0 + sc2) + sh2
        y = moe_ffn(h, moe_w_router[l], moe_b_router[l], moe_w_gu[l], moe_b_gu[l], moe_w_down[l], moe_b_down[l])
        x = layer_norm(ALPHA * x + (1.0 + g2) * y, ln_g[l, 1], ln_b[l, 1])
    return x
```

```python
import functools

import numpy as np
import jax
import jax.numpy as jnp
from jax import lax
from jax.experimental import pallas as pl
from jax.experimental.pallas import tpu as pltpu

F32 = jnp.float32
BF16 = jnp.bfloat16
I32 = jnp.int32

DEPTH = 4
ALPHA = (2.0 * DEPTH) ** 0.25
LN_EPS = 1e-5

GLA_HEADS = 4
GLA_HK = 128
GLA_HV = 256
GLA_DK = GLA_HEADS * GLA_HK
GLA_DV = GLA_HEADS * GLA_HV
GLA_RANK = 16
GLA_TAU = 16.0
GLA_CHUNK = 64
GLA_ROWS = 512

RG_BW = 128
RG_BLOCKS = 10
RG_WIDTH = RG_BW * RG_BLOCKS
RG_CONV = 4
RG_C = 8.0
RG_ROWS = 256

FOX_HD = 64
FOX_HEADS = 16
FOX_PAIRS = FOX_HEADS // 2
FOX_TQ = 512
FOX_PREP_ROWS = 256
FOX_AUG = 256

N_EXPERTS = 32
TOP_K = 4
SWIGLU_LIMIT = 7.0
SWIGLU_ALPHA = 1.702
MOE_ROWS = 256
NEG_BIG = -1e30

LANES = 128
VMEM_LIMIT = 56 * 1024 * 1024


def _cparams(sem, vmem=VMEM_LIMIT):
    return pltpu.CompilerParams(dimension_semantics=sem, vmem_limit_bytes=vmem)


def _softplus(z):
    return jnp.maximum(z, 0.0) + jnp.log(1.0 + jnp.exp(-jnp.abs(z)))


def _layer_norm_rows(z, g, b):
    mu = jnp.mean(z, axis=-1, keepdims=True)
    zc = z - mu
    var = jnp.mean(zc * zc, axis=-1, keepdims=True)
    return zc * lax.rsqrt(var + LN_EPS) * g + b


def _ada_kernel(c_ref, w_ref, b_ref, o_ref):
    c = c_ref[...]
    ca = c * jax.nn.sigmoid(c)
    o_ref[0] = jnp.dot(ca.astype(BF16), w_ref[0].astype(BF16), preferred_element_type=F32) + b_ref[0]


def _ada_call(c, ada_w, ada_b):
    depth, d, n = ada_w.shape
    b = c.shape[0]
    tn = n // 4
    return pl.pallas_call(
        _ada_kernel,
        grid=(depth, n // tn),
        in_specs=[
            pl.BlockSpec((b, d), lambda l, j: (0, 0)),
            pl.BlockSpec((1, d, tn), lambda l, j: (l, 0, j)),
            pl.BlockSpec((1, 1, tn), lambda l, j: (l, 0, j)),
        ],
        out_specs=pl.BlockSpec((1, b, tn), lambda l, j: (l, 0, j)),
        out_shape=jax.ShapeDtypeStruct((depth, b, n), F32),
        compiler_params=_cparams(("parallel", "parallel")),
        name="ada_mod",
    )(c, ada_w, ada_b.reshape(depth, 1, n))


def _modmm_kernel(x_ref, sc_ref, sh_ref, w_ref, o_ref):
    h = x_ref[...] * (1.0 + sc_ref[0]) + sh_ref[0]
    o_ref[...] = jnp.dot(h.astype(BF16), w_ref[...], preferred_element_type=F32)


def _modmm_call(x, sc, sh, w_bf, seq, tm):
    t, d = x.shape
    n = w_bf.shape[1]
    per_seq = seq // tm
    return pl.pallas_call(
        _modmm_kernel,
        grid=(t // tm,),
        in_specs=[
            pl.BlockSpec((tm, d), lambda i: (i, 0)),
            pl.BlockSpec((1, 1, d), lambda i: (i // per_seq, 0, 0)),
            pl.BlockSpec((1, 1, d), lambda i: (i // per_seq, 0, 0)),
            pl.BlockSpec((d, n), lambda i: (0, 0)),
        ],
        out_specs=pl.BlockSpec((tm, n), lambda i: (i, 0)),
        out_shape=jax.ShapeDtypeStruct((t, n), F32),
        compiler_params=_cparams(("parallel",)),
        name="mod_inproj",
    )(x, sc, sh, w_bf)


def _outln_kernel(y_ref, w_ref, x_ref, g_ref, lg_ref, lb_ref, o_ref):
    y = jnp.dot(y_ref[...].astype(BF16), w_ref[...], preferred_element_type=F32)
    z = ALPHA * x_ref[...] + (1.0 + g_ref[0]) * y
    o_ref[...] = _layer_norm_rows(z, lg_ref[...], lb_ref[...])


def _outln_call(y, w_bf, x, gate, ln_g, ln_b, seq, tm):
    t, k = y.shape
    d = x.shape[1]
    per_seq = seq // tm
    return pl.pallas_call(
        _outln_kernel,
        grid=(t // tm,),
        in_specs=[
            pl.BlockSpec((tm, k), lambda i: (i, 0)),
            pl.BlockSpec((k, d), lambda i: (0, 0)),
            pl.BlockSpec((tm, d), lambda i: (i, 0)),
            pl.BlockSpec((1, 1, d), lambda i: (i // per_seq, 0, 0)),
            pl.BlockSpec((1, d), lambda i: (0, 0)),
            pl.BlockSpec((1, d), lambda i: (0, 0)),
        ],
        out_specs=pl.BlockSpec((tm, d), lambda i: (i, 0)),
        out_shape=jax.ShapeDtypeStruct((t, d), F32),
        compiler_params=_cparams(("parallel",)),
        name="outproj_ln",
    )(y, w_bf, x, gate, ln_g.reshape(1, d), ln_b.reshape(1, d))


def _gla_kernel(q_ref, k_ref, v_ref, r_ref, g_ref, wg_ref, bg_ref, ng_ref, tri_ref,
                o_ref, la_ref, st_ref):
    c_rows = GLA_CHUNK
    half = c_rows // 2

    @pl.when(pl.program_id(1) == 0)
    def _():
        st_ref[...] = jnp.zeros_like(st_ref)

    z = jnp.dot(g_ref[...].astype(BF16), wg_ref[...], preferred_element_type=F32) + bg_ref[...]
    la_ref[...] = -_softplus(-z) * (1.0 / GLA_TAU)

    row = lax.broadcasted_iota(I32, (c_rows, c_rows), 0)
    col = lax.broadcasted_iota(I32, (c_rows, c_rows), 1)
    causal = col <= row
    tri = tri_ref[...]
    n_chunks = q_ref.shape[0] // c_rows

    def chunk(c, carry):
        r0 = pl.multiple_of(c * c_rows, c_rows)
        rows = pl.ds(r0, c_rows)
        bc = jnp.dot(tri, la_ref[rows, :], preferred_element_type=F32, precision=lax.Precision.HIGHEST)
        b_mid = bc[half - 1:half, :]
        b_last = bc[c_rows - 1:c_rows, :]
        e_q_mid = jnp.exp(bc - b_mid)
        e_k_mid = jnp.exp(b_mid - bc)
        e_q = jnp.exp(bc)
        e_k_end = jnp.exp(b_last - bc)
        dec = jnp.exp(b_last)
        qc = q_ref[rows, :] * (GLA_HK ** -0.5)
        kc = k_ref[rows, :]
        for h in range(GLA_HEADS):
            ks = slice(h * GLA_HK, (h + 1) * GLA_HK)
            vs = slice(h * GLA_HV, (h + 1) * GLA_HV)
            qm = (qc[:, ks] * e_q_mid[:, ks]).astype(BF16)
            km = (kc[:, ks] * e_k_mid[:, ks]).astype(BF16)
            s = lax.dot_general(qm, km, (((1,), (1,)), ((), ())), preferred_element_type=F32)
            s = jnp.where(causal, s, 0.0)
            vh = v_ref[rows, vs].astype(BF16)
            o = jnp.dot(s.astype(BF16), vh, preferred_element_type=F32)
            qd = (qc[:, ks] * e_q[:, ks]).astype(BF16)
            st = st_ref[h]
            o = o + lax.dot_general(qd, st.astype(BF16), (((1,), (1,)), ((), ())),
                                    preferred_element_type=F32)
            ke = (kc[:, ks] * e_k_end[:, ks]).astype(BF16)
            st_ref[h] = st * dec[:, ks] + lax.dot_general(
                vh, ke, (((0,), (0,)), ((), ())), preferred_element_type=F32)
            ms = jnp.mean(o * o, axis=-1, keepdims=True)
            y = o * lax.rsqrt(ms + LN_EPS) * ng_ref[:, vs]
            rr = r_ref[rows, vs]
            o_ref[rows, vs] = y * (rr * jax.nn.sigmoid(rr))
        return carry

    lax.fori_loop(0, n_chunks, chunk, 0)


def _gla_call(proj, wg_pad_bf, b_gate, norm_g, batch, seq):
    t = proj.shape[0]
    rows = min(GLA_ROWS, seq)
    per_seq = seq // rows
    tri = jnp.asarray(np.tril(np.ones((GLA_CHUNK, GLA_CHUNK), np.float32)))
    row_map = lambda b, j: b * per_seq + j
    return pl.pallas_call(
        _gla_kernel,
        grid=(batch, per_seq),
        in_specs=[
            pl.BlockSpec((rows, GLA_DK), lambda b, j: (row_map(b, j), 0)),
            pl.BlockSpec((rows, GLA_DK), lambda b, j: (row_map(b, j), 1)),
            pl.BlockSpec((rows, GLA_DV), lambda b, j: (row_map(b, j), 1)),
            pl.BlockSpec((rows, GLA_DV), lambda b, j: (row_map(b, j), 2)),
            pl.BlockSpec((rows, LANES), lambda b, j: (row_map(b, j), (2 * GLA_DK + 2 * GLA_DV) // LANES)),
            pl.BlockSpec((LANES, GLA_DK), lambda b, j: (0, 0)),
            pl.BlockSpec((1, GLA_DK), lambda b, j: (0, 0)),
            pl.BlockSpec((1, GLA_DV), lambda b, j: (0, 0)),
            pl.BlockSpec((GLA_CHUNK, GLA_CHUNK), lambda b, j: (0, 0)),
        ],
        out_specs=pl.BlockSpec((rows, GLA_DV), lambda b, j: (row_map(b, j), 0)),
        out_shape=jax.ShapeDtypeStruct((t, GLA_DV), F32),
        scratch_shapes=[
            pltpu.VMEM((rows, GLA_DK), F32),
            pltpu.VMEM((GLA_HEADS, GLA_HV, GLA_HK), F32),
        ],
        compiler_params=_cparams(("parallel", "arbitrary")),
        name="gla_core",
    )(proj, proj, proj, proj, proj, wg_pad_bf, b_gate.reshape(1, GLA_DK), norm_g.reshape(1, GLA_DV), tri)


def _rg_kernel(y_ref, x_ref, cw_ref, cb_ref, wg_ref, brg_ref, big_ref, lam_ref,
               o_ref, xext_ref, hcar_ref):
    rows = x_ref.shape[0]

    @pl.when(pl.program_id(1) == 0)
    def _():
        xext_ref[0:8, :] = jnp.zeros((8, RG_WIDTH), F32)
        hcar_ref[...] = jnp.zeros_like(hcar_ref)

    xext_ref[8:8 + rows, :] = x_ref[...]
    xc = cb_ref[...] + xext_ref[pl.ds(8 - (RG_CONV - 1), rows), :] * cw_ref[0:1, :]
    for j in range(1, RG_CONV):
        xc = xc + xext_ref[pl.ds(8 - (RG_CONV - 1) + j, rows), :] * cw_ref[j:j + 1, :]
    xext_ref[0:8, :] = x_ref[rows - 8:rows, :]

    r_parts, i_parts = [], []
    for n in range(RG_BLOCKS):
        cs = slice(n * RG_BW, (n + 1) * RG_BW)
        g = jnp.dot(xc[:, cs].astype(BF16), wg_ref[n], preferred_element_type=F32)
        r_parts.append(g[:, :RG_BW])
        i_parts.append(g[:, RG_BW:])
    r = jax.nn.sigmoid(jnp.concatenate(r_parts, axis=-1) + brg_ref[...])
    ig = jax.nn.sigmoid(jnp.concatenate(i_parts, axis=-1) + big_ref[...])

    log_a = -RG_C * r * _softplus(-lam_ref[...])
    a = jnp.exp(log_a)
    th = jnp.tanh(log_a)
    u = jnp.sqrt(-2.0 * th / (1.0 - th)) * (ig * xc)

    ridx = lax.broadcasted_iota(I32, (rows, 1), 0)
    acc_a, acc_h = a, u
    d = 1
    while d < rows:
        keep = ridx >= d
        sh_a = pltpu.roll(acc_a, d, axis=0)
        sh_h = pltpu.roll(acc_h, d, axis=0)
        acc_h = jnp.where(keep, acc_a * sh_h + acc_h, acc_h)
        acc_a = jnp.where(keep, acc_a * sh_a, acc_a)
        d *= 2
    h = acc_h + acc_a * hcar_ref[...]
    hcar_ref[...] = h[rows - 1:rows, :]

    yb = y_ref[...]
    gelu = 0.5 * yb * (1.0 + jnp.tanh(0.7978845608028654 * (yb + 0.044715 * (yb * yb * yb))))
    o_ref[...] = h * gelu


def _rg_call(proj, conv_w, conv_b, wgate_bf, b_rg, b_ig, lam, batch, seq):
    t = proj.shape[0]
    rows = min(RG_ROWS, seq)
    per_seq = seq // rows
    row_map = lambda b, j: b * per_seq + j
    vec = lambda: pl.BlockSpec((1, RG_WIDTH), lambda b, j: (0, 0))
    return pl.pallas_call(
        _rg_kernel,
        grid=(batch, per_seq),
        in_specs=[
            pl.BlockSpec((rows, RG_WIDTH), lambda b, j: (row_map(b, j), 0)),
            pl.BlockSpec((rows, RG_WIDTH), lambda b, j: (row_map(b, j), 1)),
            pl.BlockSpec((RG_CONV, RG_WIDTH), lambda b, j: (0, 0)),
            vec(),
            pl.BlockSpec((RG_BLOCKS, RG_BW, 2 * RG_BW), lambda b, j: (0, 0, 0)),
            vec(), vec(), vec(),
        ],
        out_specs=pl.BlockSpec((rows, RG_WIDTH), lambda b, j: (row_map(b, j), 0)),
        out_shape=jax.ShapeDtypeStruct((t, RG_WIDTH), F32),
        scratch_shapes=[
            pltpu.VMEM((rows + 8, RG_WIDTH), F32),
            pltpu.VMEM((1, RG_WIDTH), F32),
        ],
        compiler_params=_cparams(("parallel", "arbitrary")),
        name="rglru_core",
    )(proj, proj, conv_w, conv_b.reshape(1, RG_WIDTH), wgate_bf, b_rg.reshape(1, RG_WIDTH),
      b_ig.reshape(1, RG_WIDTH), lam.reshape(1, RG_WIDTH))


def _fox_bias_selectors():
    piece = lambda part, head: part * FOX_HEADS + head
    one = 3 * FOX_HEADS
    sel_q = np.zeros((LANES, FOX_PAIRS * LANES), np.float32)
    sel_ka = np.zeros((LANES, FOX_PAIRS * LANES), np.float32)
    sel_kb = np.zeros((LANES, FOX_PAIRS * LANES), np.float32)
    for p in range(FOX_PAIRS):
        base = p * LANES
        for side, head in enumerate((2 * p, 2 * p + 1)):
            sel_k = sel_ka if side == 0 else sel_kb
            for part in range(3):
                sel_q[piece(part, head), base + 6 * side + part] = 1.0
                sel_q[one, base + 6 * side + 3 + part] = 1.0
                sel_k[one, base + 6 * side + part] = 1.0
                sel_k[piece(part, head), base + 6 * side + 3 + part] = -1.0
    return sel_q, sel_ka, sel_kb


def _fox_prep_kernel(q_ref, k_ref, v_ref, f_ref, bf_ref, gq_ref, gk_ref, grp_ref, tri_ref,
                     sq_ref, ska_ref, skb_ref, qa_ref, ka_ref, kb_ref, vb_ref, dcar_ref):
    @pl.when(pl.program_id(1) == 0)
    def _():
        dcar_ref[...] = jnp.zeros_like(dcar_ref)

    lane = lax.broadcasted_iota(I32, (1, LANES), 1)
    log_f = -_softplus(-(f_ref[...] + bf_ref[...]))
    log_f = jnp.where(lane < FOX_HEADS, log_f, 0.0)
    dcum = jnp.dot(tri_ref[...], log_f, preferred_element_type=F32,
                   precision=lax.Precision.HIGHEST) + dcar_ref[...]
    dcar_ref[...] = dcum[dcum.shape[0] - 1:, :]

    d_hi = dcum.astype(BF16)
    rem = dcum - d_hi.astype(F32)
    d_mid = rem.astype(BF16)
    d_lo = (rem - d_mid.astype(F32)).astype(BF16)
    hi, mid, lo = (pltpu.roll(p.astype(F32), s, axis=1) for p, s in
                   ((d_hi, 0), (d_mid, FOX_HEADS), (d_lo, 2 * FOX_HEADS)))
    pieces = jnp.where(lane < FOX_HEADS, hi,
                       jnp.where(lane < 2 * FOX_HEADS, mid,
                                 jnp.where(lane < 3 * FOX_HEADS, lo,
                                           jnp.where(lane == 3 * FOX_HEADS, 1.0, 0.0)))).astype(BF16)
    bias_q = jnp.dot(pieces, sq_ref[...], preferred_element_type=F32).astype(BF16)
    bias_ka = jnp.dot(pieces, ska_ref[...], preferred_element_type=F32).astype(BF16)
    bias_kb = jnp.dot(pieces, skb_ref[...], preferred_element_type=F32).astype(BF16)

    grp = grp_ref[...]
    first_head = lane < FOX_HD

    def rms(xt, gain):
        sq = xt * xt
        sq_hi = sq.astype(BF16)
        sq_lo = (sq - sq_hi.astype(F32)).astype(BF16)
        ssum = (jnp.dot(sq_hi, grp, preferred_element_type=F32)
                + jnp.dot(sq_lo, grp, preferred_element_type=F32))
        return xt * lax.rsqrt(ssum * (1.0 / FOX_HD) + LN_EPS) * gain

    for p in range(FOX_PAIRS):
        src = slice(p * LANES, (p + 1) * LANES)
        dst = slice(p * FOX_AUG, p * FOX_AUG + LANES)
        dst_bias = slice(p * FOX_AUG + LANES, (p + 1) * FOX_AUG)
        qn = rms(q_ref[:, src], gq_ref[...] * (FOX_HD ** -0.5)).astype(BF16)
        kn = rms(k_ref[:, src], gk_ref[...]).astype(BF16)
        qa_ref[:, dst] = qn
        qa_ref[:, dst_bias] = bias_q[:, src]
        ka_ref[:, dst] = jnp.where(first_head, kn, jnp.zeros_like(kn))
        ka_ref[:, dst_bias] = bias_ka[:, src]
        kb_ref[:, dst] = jnp.where(first_head, jnp.zeros_like(kn), kn)
        kb_ref[:, dst_bias] = bias_kb[:, src]
    vb_ref[...] = v_ref[...].astype(BF16)


def _fox_prep_call(proj, b_f, q_norm_g, k_norm_g, batch, seq):
    t = proj.shape[0]
    d = FOX_HEADS * FOX_HD
    rows = min(FOX_PREP_ROWS, seq)
    per_seq = seq // rows
    row_map = lambda b, j: b * per_seq + j
    sel_q, sel_ka, sel_kb = (jnp.asarray(s, BF16) for s in _fox_bias_selectors())
    grp = np.kron(np.eye(2, dtype=np.float32), np.ones((FOX_HD, FOX_HD), np.float32))
    tri = np.tril(np.ones((rows, rows), np.float32))
    bf_pad = jnp.zeros((1, LANES), F32).at[0, :FOX_HEADS].set(b_f)
    gq2 = jnp.tile(q_norm_g, 2).reshape(1, LANES)
    gk2 = jnp.tile(k_norm_g, 2).reshape(1, LANES)
    const = lambda shape: pl.BlockSpec(shape, lambda b, j: tuple(0 for _ in shape))
    aug = jax.ShapeDtypeStruct((t, FOX_PAIRS * FOX_AUG), BF16)
    aug_spec = pl.BlockSpec((rows, FOX_PAIRS * FOX_AUG), lambda b, j: (row_map(b, j), 0))
    return pl.pallas_call(
        _fox_prep_kernel,
        grid=(batch, per_seq),
        in_specs=[
            pl.BlockSpec((rows, d), lambda b, j: (row_map(b, j), 0)),
            pl.BlockSpec((rows, d), lambda b, j: (row_map(b, j), 1)),
            pl.BlockSpec((rows, d), lambda b, j: (row_map(b, j), 2)),
            pl.BlockSpec((rows, LANES), lambda b, j: (row_map(b, j), 4 * d // LANES)),
            const((1, LANES)), const((1, LANES)), const((1, LANES)),
            const((LANES, LANES)), const((rows, rows)),
            const((LANES, FOX_PAIRS * LANES)), const((LANES, FOX_PAIRS * LANES)),
            const((LANES, FOX_PAIRS * LANES)),
        ],
        out_specs=[aug_spec, aug_spec, aug_spec,
                   pl.BlockSpec((rows, d), lambda b, j: (row_map(b, j), 0))],
        out_shape=[aug, aug, aug, jax.ShapeDtypeStruct((t, d), BF16)],
        scratch_shapes=[pltpu.VMEM((1, LANES), F32)],
        compiler_params=_cparams(("parallel", "arbitrary")),
        name="fox_prep",
    )(proj, proj, proj, proj, bf_pad, gq2, gk2, jnp.asarray(grp, BF16), jnp.asarray(tri),
      sel_q, sel_ka, sel_kb)


def _fox_attn_kernel(qa_ref, ka_ref, kb_ref, v_ref, og_ref, o_ref, m_ref, l_ref, acc_ref):
    qi = pl.program_id(2)
    ki = pl.program_id(3)
    tq = qa_ref.shape[0]
    lane = lax.broadcasted_iota(I32, (1, LANES), 1)
    first_head = lane < FOX_HD

    @pl.when(ki == 0)
    def _():
        m_ref[...] = jnp.full_like(m_ref, NEG_BIG)
        l_ref[...] = jnp.zeros_like(l_ref)
        acc_ref[...] = jnp.zeros_like(acc_ref)

    def step(masked):
        q = qa_ref[...]
        v = v_ref[...]
        pv, alpha = [], []
        for side, k_ref in enumerate((ka_ref, kb_ref)):
            s = lax.dot_general(q, k_ref[...], (((1,), (1,)), ((), ())), preferred_element_type=F32)
            if masked:
                row = lax.broadcasted_iota(I32, s.shape, 0)
                col = lax.broadcasted_iota(I32, s.shape, 1)
                s = jnp.where(col <= row, s, NEG_BIG)
            m_prev = m_ref[side]
            m_new = jnp.maximum(m_prev, jnp.max(s, axis=-1, keepdims=True))
            a = jnp.exp(m_prev - m_new)
            p = jnp.exp(s - m_new)
            l_ref[side] = a * l_ref[side] + jnp.sum(p, axis=-1, keepdims=True)
            m_ref[side] = m_new
            pv.append(jnp.dot(p.astype(BF16), v, preferred_element_type=F32))
            alpha.append(a)
        acc_ref[...] = (acc_ref[...] * jnp.where(first_head, alpha[0], alpha[1])
                        + jnp.where(first_head, pv[0], pv[1]))

    @pl.when(ki < qi)
    def _():
        step(False)

    @pl.when(ki == qi)
    def _():
        step(True)
        og = og_ref[...]
        o_ref[...] = acc_ref[...] / jnp.where(first_head, l_ref[0], l_ref[1]) * jax.nn.sigmoid(og)


def _fox_attn_call(qa, ka, kb, vb, proj, batch, seq):
    t = qa.shape[0]
    d = FOX_HEADS * FOX_HD
    tq = min(FOX_TQ, seq)
    nq = seq // tq
    q_map = lambda b, p, i, j: (b * nq + i, p)
    k_map = lambda b, p, i, j: (b * nq + jnp.minimum(i, j), p)
    return pl.pallas_call(
        _fox_attn_kernel,
        grid=(batch, FOX_PAIRS, nq, nq),
        in_specs=[
            pl.BlockSpec((tq, FOX_AUG), q_map),
            pl.BlockSpec((tq, FOX_AUG), k_map),
            pl.BlockSpec((tq, FOX_AUG), k_map),
            pl.BlockSpec((tq, LANES), k_map),
            pl.BlockSpec((tq, LANES), lambda b, p, i, j: (b * nq + i, 3 * d // LANES + p)),
        ],
        out_specs=pl.BlockSpec((tq, LANES), q_map),
        out_shape=jax.ShapeDtypeStruct((t, d), F32),
        scratch_shapes=[
            pltpu.VMEM((2, tq, 1), F32),
            pltpu.VMEM((2, tq, 1), F32),
            pltpu.VMEM((tq, LANES), F32),
        ],
        compiler_params=_cparams(("parallel", "parallel", "parallel", "arbitrary")),
        name="fox_attn",
    )(qa, ka, kb, vb, proj)


def _router_kernel(x_ref, sc_ref, sh_ref, whi_ref, wlo_ref, b_ref, h_ref, idx_ref, wt_ref):
    h = x_ref[...] * (1.0 + sc_ref[0]) + sh_ref[0]
    h_ref[...] = h
    h_hi = h.astype(BF16)
    h_lo = (h - h_hi.astype(F32)).astype(BF16)
    logits = (jnp.dot(h_hi, whi_ref[...], preferred_element_type=F32)
              + jnp.dot(h_lo, whi_ref[...], preferred_element_type=F32)
              + jnp.dot(h_hi, wlo_ref[...], preferred_element_type=F32)
              + b_ref[...])
    lane = lax.broadcasted_iota(I32, logits.shape, 1)
    logits = jnp.where(lane < N_EXPERTS, logits, NEG_BIG)
    idx_out = jnp.zeros(logits.shape, I32)
    wt_out = jnp.zeros(logits.shape, F32)
    top0 = None
    denom = None
    for k in range(TOP_K):
        m = jnp.max(logits, axis=-1, keepdims=True)
        sel = jnp.min(jnp.where(logits == m, lane, LANES), axis=-1, keepdims=True)
        logits = jnp.where(lane == sel, NEG_BIG, logits)
        if k == 0:
            top0 = m
        e = jnp.exp(m - top0)
        denom = e if k == 0 else denom + e
        idx_out = jnp.where(lane == k, sel, idx_out)
        wt_out = jnp.where(lane == k, e, wt_out)
    idx_ref[...] = idx_out
    wt_ref[...] = wt_out / denom


def _router_call(x, sc, sh, w_hi, w_lo, b_pad, seq, tm):
    t, d = x.shape
    per_seq = seq // tm
    const = lambda shape: pl.BlockSpec(shape, lambda i: tuple(0 for _ in shape))
    return pl.pallas_call(
        _router_kernel,
        grid=(t // tm,),
        in_specs=[
            pl.BlockSpec((tm, d), lambda i: (i, 0)),
            pl.BlockSpec((1, 1, d), lambda i: (i // per_seq, 0, 0)),
            pl.BlockSpec((1, 1, d), lambda i: (i // per_seq, 0, 0)),
            const((d, LANES)), const((d, LANES)), const((1, LANES)),
        ],
        out_specs=[
            pl.BlockSpec((tm, d), lambda i: (i, 0)),
            pl.BlockSpec((tm, LANES), lambda i: (i, 0)),
            pl.BlockSpec((tm, LANES), lambda i: (i, 0)),
        ],
        out_shape=[
            jax.ShapeDtypeStruct((t, d), F32),
            jax.ShapeDtypeStruct((t, LANES), I32),
            jax.ShapeDtypeStruct((t, LANES), F32),
        ],
        compiler_params=_cparams(("parallel",)),
        name="moe_router",
    )(x, sc, sh, w_hi, w_lo, b_pad)


MOE_GROUP = 2 * LANES


def _moe_kernel(be_ref, src_ref, dst_ref, nact_ref,
                h_hbm, wgu_ref, bgu_ref, wdn_ref, bdn_ref, perm_ref,
                y_hbm,
                xbuf, ybuf, wgu_bf, wdn_bf, gsem, ssem):
    i = pl.program_id(0)
    n_act = nact_ref[0]
    rows = MOE_ROWS
    slot = i % 2
    n_groups = wgu_ref.shape[2] // MOE_GROUP

    def gather_copy(blk, r, sl):
        tok = src_ref[blk * rows + r]
        return pltpu.make_async_copy(h_hbm.at[pl.ds(tok, 1), :], xbuf.at[sl, pl.ds(r, 1), :], gsem.at[sl])

    def scatter_copy(blk, r, sl):
        row = dst_ref[blk * rows + r]
        return pltpu.make_async_copy(ybuf.at[sl, pl.ds(r, 1), :], y_hbm.at[pl.ds(row, 1), :], ssem.at[sl])

    def start_rows(copy_fn, blk, sl):
        def body(r, c):
            copy_fn(blk, r, sl).start()
            return c
        lax.fori_loop(0, rows, body, 0)

    def wait_gather(sl):
        pltpu.make_async_copy(xbuf.at[sl], xbuf.at[sl], gsem.at[sl]).wait()

    def wait_scatter(sl):
        pltpu.make_async_copy(ybuf.at[sl], ybuf.at[sl], ssem.at[sl]).wait()

    @pl.when(i == 0)
    def _():
        start_rows(gather_copy, 0, 0)
        n_dump = 2 * rows
        ybuf[...] = jnp.zeros_like(ybuf)
        for sl in range(2):
            zero_dump = pltpu.make_async_copy(
                ybuf.at[sl], y_hbm.at[pl.ds(y_hbm.shape[0] - n_dump + sl * rows, rows), :], ssem.at[sl])
            zero_dump.start()
            zero_dump.wait()

    @pl.when(i < n_act)
    def _():
        @pl.when(i + 1 < n_act)
        def _():
            start_rows(gather_copy, i + 1, 1 - slot)

        e_changed = jnp.logical_or(i == 0, be_ref[i] != be_ref[jnp.maximum(i - 1, 0)])

        @pl.when(e_changed)
        def _():
            perm = perm_ref[...]
            for g in range(n_groups):
                cs = slice(g * MOE_GROUP, (g + 1) * MOE_GROUP)
                wgu_bf[:, cs] = jnp.dot(wgu_ref[0, :, cs].astype(BF16), perm,
                                        preferred_element_type=F32).astype(BF16)
            wdn_bf[...] = wdn_ref[0].astype(BF16)

        wait_gather(slot)
        x = xbuf[slot].astype(BF16)
        gu = jnp.dot(x, wgu_bf[...], preferred_element_type=F32) + bgu_ref[0]
        parts = []
        for g in range(n_groups):
            gate = jnp.minimum(gu[:, g * MOE_GROUP:g * MOE_GROUP + LANES], SWIGLU_LIMIT)
            up = jnp.clip(gu[:, g * MOE_GROUP + LANES:(g + 1) * MOE_GROUP], -SWIGLU_LIMIT, SWIGLU_LIMIT)
            glu = gate * jax.nn.sigmoid(SWIGLU_ALPHA * gate)
            parts.append(((up + 1.0) * glu).astype(BF16))
        hmid = jnp.concatenate(parts, axis=-1)
        y = jnp.dot(hmid, wdn_bf[...], preferred_element_type=F32) + bdn_ref[0]

        @pl.when(i >= 2)
        def _():
            wait_scatter(slot)

        ybuf[slot] = y
        start_rows(scatter_copy, i, slot)

        @pl.when(i == n_act - 1)
        def _():
            wait_scatter(slot)

            @pl.when(i >= 1)
            def _():
                wait_scatter(1 - slot)


def _moe_call(h, be, src, dst, nact, w_gu, b_gu, w_down, b_down, n_blocks, n_out_rows):
    t, d = h.shape
    n_exp, _, f2 = w_gu.shape
    f = w_down.shape[1]
    perm = np.zeros((MOE_GROUP, MOE_GROUP), np.float32)
    for jj in range(LANES):
        perm[2 * jj, jj] = 1.0
        perm[2 * jj + 1, LANES + jj] = 1.0
    b_gu_grouped = b_gu.reshape(n_exp, f2 // MOE_GROUP, LANES, 2).transpose(0, 1, 3, 2).reshape(n_exp, 1, f2)
    grid_spec = pltpu.PrefetchScalarGridSpec(
        num_scalar_prefetch=4,
        grid=(n_blocks,),
        in_specs=[
            pl.BlockSpec(memory_space=pl.ANY),
            pl.BlockSpec((1, d, f2), lambda i, be, s, ds_, na: (be[i], 0, 0)),
            pl.BlockSpec((1, 1, f2), lambda i, be, s, ds_, na: (be[i], 0, 0)),
            pl.BlockSpec((1, f, d), lambda i, be, s, ds_, na: (be[i], 0, 0)),
            pl.BlockSpec((1, 1, d), lambda i, be, s, ds_, na: (be[i], 0, 0)),
            pl.BlockSpec((MOE_GROUP, MOE_GROUP), lambda i, be, s, ds_, na: (0, 0)),
        ],
        out_specs=pl.BlockSpec(memory_space=pl.ANY),
        scratch_shapes=[
            pltpu.VMEM((2, MOE_ROWS, d), F32),
            pltpu.VMEM((2, MOE_ROWS, d), F32),
            pltpu.VMEM((d, f2), BF16),
            pltpu.VMEM((f, d), BF16),
            pltpu.SemaphoreType.DMA((2,)),
            pltpu.SemaphoreType.DMA((2,)),
        ],
    )
    return pl.pallas_call(
        _moe_kernel,
        grid_spec=grid_spec,
        out_shape=jax.ShapeDtypeStruct((n_out_rows, d), F32),
        compiler_params=_cparams(("arbitrary",)),
        name="moe_experts",
    )(be, src, dst, nact, h, w_gu, b_gu_grouped, w_down, b_down.reshape(n_exp, 1, d),
      jnp.asarray(perm, BF16))


def _combine_kernel(y0_ref, y1_ref, y2_ref, y3_ref, wt_ref, x_ref, g_ref, lg_ref, lb_ref, o_ref):
    wt = wt_ref[...]
    y = y0_ref[...] * wt[:, 0:1]
    for k, y_ref in enumerate((y1_ref, y2_ref, y3_ref), start=1):
        y = y + y_ref[...] * wt[:, k:k + 1]
    z = ALPHA * x_ref[...] + (1.0 + g_ref[0]) * y
    o_ref[...] = _layer_norm_rows(z, lg_ref[...], lb_ref[...])


def _combine_call(y4, wt, x, gate, ln_g, ln_b, seq, tm):
    t, d = x.shape
    per_seq = seq // tm
    nblk = t // tm
    y_spec = lambda k: pl.BlockSpec((tm, d), lambda i, k=k: (k * nblk + i, 0))
    return pl.pallas_call(
        _combine_kernel,
        grid=(nblk,),
        in_specs=[
            y_spec(0), y_spec(1), y_spec(2), y_spec(3),
            pl.BlockSpec((tm, LANES), lambda i: (i, 0)),
            pl.BlockSpec((tm, d), lambda i: (i, 0)),
            pl.BlockSpec((1, 1, d), lambda i: (i // per_seq, 0, 0)),
            pl.BlockSpec((1, d), lambda i: (0, 0)),
            pl.BlockSpec((1, d), lambda i: (0, 0)),
        ],
        out_specs=pl.BlockSpec((tm, d), lambda i: (i, 0)),
        out_shape=jax.ShapeDtypeStruct((t, d), F32),
        compiler_params=_cparams(("parallel",)),
        name="moe_combine_ln",
    )(y4, y4, y4, y4, wt, x, gate, ln_g.reshape(1, d), ln_b.reshape(1, d))


def _moe_plan(top_idx, t):
    n_assign = t * TOP_K
    n_blocks = -(-(n_assign + N_EXPERTS * (MOE_ROWS - 1)) // MOE_ROWS)
    n_pad = n_blocks * MOE_ROWS
    flat_e = top_idx.reshape(-1)
    order = jnp.argsort(flat_e).astype(I32)
    sorted_e = flat_e[order]
    counts = jnp.bincount(flat_e, length=N_EXPERTS).astype(I32)
    starts = jnp.cumsum(counts) - counts
    padded = (counts + MOE_ROWS - 1) // MOE_ROWS * MOE_ROWS
    pad_ends = jnp.cumsum(padded)
    pad_starts = pad_ends - padded
    slot = pad_starts[sorted_e] + jnp.arange(n_assign, dtype=I32) - starts[sorted_e]
    pos = jnp.arange(n_pad, dtype=I32)
    dump = n_assign + ((pos // MOE_ROWS) % 2) * MOE_ROWS + pos % MOE_ROWS
    src = jnp.zeros((n_pad,), I32).at[slot].set(order // TOP_K)
    dst = dump.at[slot].set((order % TOP_K) * t + order // TOP_K)
    block_e = jnp.minimum(
        jnp.searchsorted(pad_ends, jnp.arange(n_blocks, dtype=I32) * MOE_ROWS, side='right'),
        N_EXPERTS - 1).astype(I32)
    n_act = (pad_ends[-1] // MOE_ROWS).astype(I32).reshape(1)
    return block_e, src, dst, n_act, n_blocks, n_assign + 2 * MOE_ROWS


def _moe_layer(x, sc, sh, gate, ln_g, ln_b, w_router, b_router, w_gu, b_gu, w_down, b_down, seq, tm):
    t, d = x.shape
    w_pad = jnp.zeros((d, LANES), F32).at[:, :N_EXPERTS].set(w_router)
    w_hi = w_pad.astype(BF16)
    w_lo = (w_pad - w_hi.astype(F32)).astype(BF16)
    b_pad = jnp.zeros((1, LANES), F32).at[0, :N_EXPERTS].set(b_router)
    h, idx, wt = _router_call(x, sc, sh, w_hi, w_lo, b_pad, seq, tm)
    block_e, src, dst, n_act, n_blocks, n_rows = _moe_plan(idx[:, :TOP_K], t)
    y4 = _moe_call(h, block_e, src, dst, n_act, w_gu, b_gu, w_down, b_down, n_blocks, n_rows)
    return _combine_call(y4, wt, x, gate, ln_g, ln_b, seq, tm)


def _pad_cols(w, n):
    return jnp.pad(w, ((0, 0), (0, n - w.shape[1])))


def kernel(x, c, ada_w, ada_b, ln_g, ln_b, gla_w_in, gla_w_gate_up, gla_b_gate, gla_norm_g, gla_w_out, rg_w_in, rg_conv_w, rg_conv_b, rg_w_rg, rg_b_rg, rg_w_ig, rg_b_ig, rg_lambda, rg_w_out, fox_w_in, fox_b_f, fox_q_norm_g, fox_k_norm_g, fox_w_out, moe_w_router, moe_b_router, moe_w_gu, moe_b_gu, moe_w_down, moe_b_down):
    batch, seq, d = x.shape
    t = batch * seq
    tm = min(512, seq)
    depth = ada_w.shape[0]
    xt = x.reshape(t, d)
    mod = _ada_call(c, ada_w, ada_b)

    for l in range(depth):
        part = lambda i: mod[l, :, i * d:(i + 1) * d].reshape(batch, 1, d)
        sh1, sc1, g1, sh2, sc2, g2 = (part(i) for i in range(6))
        kind, j = l % 3, l // 3
        if kind == 0:
            n_in = 2 * GLA_DK + 2 * GLA_DV + LANES
            proj = _modmm_call(xt, sc1, sh1, _pad_cols(gla_w_in[j], n_in).astype(BF16), seq, tm)
            wg = jnp.pad(gla_w_gate_up[j], ((0, LANES - GLA_RANK), (0, 0))).astype(BF16)
            y = _gla_call(proj, wg, gla_b_gate[j], gla_norm_g[j], batch, seq)
            w_out = gla_w_out[j]
        elif kind == 1:
            proj = _modmm_call(xt, sc1, sh1, rg_w_in[j].astype(BF16), seq, tm)
            wgate = jnp.concatenate([rg_w_rg[j], rg_w_ig[j]], axis=-1).astype(BF16)
            y = _rg_call(proj, rg_conv_w[j], rg_conv_b[j], wgate, rg_b_rg[j], rg_b_ig[j],
                         rg_lambda[j], batch, seq)
            w_out = rg_w_out[j]
        else:
            w = fox_w_in[j]
            w = jnp.concatenate([w[:, :3 * d], w[:, 3 * d + FOX_HEADS:], w[:, 3 * d:3 * d + FOX_HEADS]], axis=1)
            proj = _modmm_call(xt, sc1, sh1, _pad_cols(w, 4 * d + LANES).astype(BF16), seq, tm)
            qa, ka, kb, vb = _fox_prep_call(proj, fox_b_f[j], fox_q_norm_g[j], fox_k_norm_g[j], batch, seq)
            y = _fox_attn_call(qa, ka, kb, vb, proj, batch, seq)
            w_out = fox_w_out[j]
        xt = _outln_call(y, w_out.astype(BF16), xt, g1, ln_g[l, 0], ln_b[l, 0], seq, tm)
        xt = _moe_layer(xt, sc2, sh2, g2, ln_g[l, 1], ln_b[l, 1], moe_w_router[l], moe_b_router[l],
                        moe_w_gu[l], moe_b_gu[l], moe_w_down[l], moe_b_down[l], seq, tm)
    return xt.reshape(batch, seq, d)
```

```python
import functools

import numpy as np
import jax
import jax.numpy as jnp
from jax import lax
from jax.experimental import pallas as pl
from jax.experimental.pallas import tpu as pltpu

F32 = jnp.float32
BF16 = jnp.bfloat16
I32 = jnp.int32

DEPTH = 4
ALPHA = (2.0 * DEPTH) ** 0.25
LN_EPS = 1e-5

GLA_HEADS = 4
GLA_HK = 128
GLA_HV = 256
GLA_DK = GLA_HEADS * GLA_HK
GLA_DV = GLA_HEADS * GLA_HV
GLA_RANK = 16
GLA_TAU = 16.0
GLA_CHUNK = 64
GLA_ROWS = 512

RG_BW = 128
RG_BLOCKS = 10
RG_WIDTH = RG_BW * RG_BLOCKS
RG_CONV = 4
RG_C = 8.0
RG_ROWS = 256

FOX_HD = 64
FOX_HEADS = 16
FOX_PAIRS = FOX_HEADS // 2
FOX_TQ = 512
FOX_PREP_ROWS = 256
FOX_AUG = 256

N_EXPERTS = 32
TOP_K = 4
SWIGLU_LIMIT = 7.0
SWIGLU_ALPHA = 1.702
MOE_ROWS = 256
NEG_BIG = -1e30
LOG2E = 1.4426950408889634

LANES = 128
VMEM_LIMIT = 56 * 1024 * 1024


def _cparams(sem, vmem=VMEM_LIMIT):
    return pltpu.CompilerParams(dimension_semantics=sem, vmem_limit_bytes=vmem)


def _softplus(z):
    return jnp.maximum(z, 0.0) + jnp.log(1.0 + jnp.exp(-jnp.abs(z)))


def _layer_norm_rows(z, g, b):
    mu = jnp.mean(z, axis=-1, keepdims=True)
    zc = z - mu
    var = jnp.mean(zc * zc, axis=-1, keepdims=True)
    return zc * lax.rsqrt(var + LN_EPS) * g + b


def _ada_kernel(c_ref, w_ref, b_ref, o_ref):
    c = c_ref[...]
    ca = c * jax.nn.sigmoid(c)
    o_ref[0] = jnp.dot(ca.astype(BF16), w_ref[0].astype(BF16), preferred_element_type=F32) + b_ref[0]


def _ada_call(c, ada_w, ada_b):
    depth, d, n = ada_w.shape
    b = c.shape[0]
    tn = n // 4
    return pl.pallas_call(
        _ada_kernel,
        grid=(depth, n // tn),
        in_specs=[
            pl.BlockSpec((b, d), lambda l, j: (0, 0)),
            pl.BlockSpec((1, d, tn), lambda l, j: (l, 0, j)),
            pl.BlockSpec((1, 1, tn), lambda l, j: (l, 0, j)),
        ],
        out_specs=pl.BlockSpec((1, b, tn), lambda l, j: (l, 0, j)),
        out_shape=jax.ShapeDtypeStruct((depth, b, n), F32),
        compiler_params=_cparams(("parallel", "parallel")),
        name="ada_mod",
    )(c, ada_w, ada_b.reshape(depth, 1, n))


def _modmm_kernel(x_ref, sc_ref, sh_ref, w_ref, o_ref):
    h = x_ref[...] * (1.0 + sc_ref[0]) + sh_ref[0]
    o_ref[...] = jnp.dot(h.astype(BF16), w_ref[...], preferred_element_type=F32)


def _modmm_call(x, sc, sh, w_bf, seq, tm):
    t, d = x.shape
    n = w_bf.shape[1]
    per_seq = seq // tm
    return pl.pallas_call(
        _modmm_kernel,
        grid=(t // tm,),
        in_specs=[
            pl.BlockSpec((tm, d), lambda i: (i, 0)),
            pl.BlockSpec((1, 1, d), lambda i: (i // per_seq, 0, 0)),
            pl.BlockSpec((1, 1, d), lambda i: (i // per_seq, 0, 0)),
            pl.BlockSpec((d, n), lambda i: (0, 0)),
        ],
        out_specs=pl.BlockSpec((tm, n), lambda i: (i, 0)),
        out_shape=jax.ShapeDtypeStruct((t, n), F32),
        compiler_params=_cparams(("parallel",)),
        name="mod_inproj",
    )(x, sc, sh, w_bf)


def _outln_kernel(y_ref, w_ref, x_ref, g_ref, lg_ref, lb_ref, o_ref):
    y = jnp.dot(y_ref[...].astype(BF16), w_ref[...], preferred_element_type=F32)
    z = ALPHA * x_ref[...] + (1.0 + g_ref[0]) * y
    o_ref[...] = _layer_norm_rows(z, lg_ref[...], lb_ref[...])


def _outln_call(y, w_bf, x, gate, ln_g, ln_b, seq, tm):
    t, k = y.shape
    d = x.shape[1]
    per_seq = seq // tm
    return pl.pallas_call(
        _outln_kernel,
        grid=(t // tm,),
        in_specs=[
            pl.BlockSpec((tm, k), lambda i: (i, 0)),
            pl.BlockSpec((k, d), lambda i: (0, 0)),
            pl.BlockSpec((tm, d), lambda i: (i, 0)),
            pl.BlockSpec((1, 1, d), lambda i: (i // per_seq, 0, 0)),
            pl.BlockSpec((1, d), lambda i: (0, 0)),
            pl.BlockSpec((1, d), lambda i: (0, 0)),
        ],
        out_specs=pl.BlockSpec((tm, d), lambda i: (i, 0)),
        out_shape=jax.ShapeDtypeStruct((t, d), F32),
        compiler_params=_cparams(("parallel",)),
        name="outproj_ln",
    )(y, w_bf, x, gate, ln_g.reshape(1, d), ln_b.reshape(1, d))


def _gla_kernel(q_ref, k_ref, v_ref, r_ref, g_ref, wg_ref, bg_ref, ng_ref, tri_ref,
                o_ref, la_ref, st_ref):
    c_rows = GLA_CHUNK
    half = c_rows // 2

    @pl.when(pl.program_id(1) == 0)
    def _():
        st_ref[...] = jnp.zeros_like(st_ref)

    z = jnp.dot(g_ref[...].astype(BF16), wg_ref[...], preferred_element_type=F32) + bg_ref[...]
    la_ref[...] = -_softplus(-z) * (1.0 / GLA_TAU)

    row = lax.broadcasted_iota(I32, (c_rows, c_rows), 0)
    col = lax.broadcasted_iota(I32, (c_rows, c_rows), 1)
    causal = col <= row
    tri = tri_ref[...]
    n_chunks = q_ref.shape[0] // c_rows

    def chunk(c, carry):
        r0 = pl.multiple_of(c * c_rows, c_rows)
        rows = pl.ds(r0, c_rows)
        bc = jnp.dot(tri, la_ref[rows, :], preferred_element_type=F32, precision=lax.Precision.HIGHEST)
        b_mid = bc[half - 1:half, :]
        b_last = bc[c_rows - 1:c_rows, :]
        e_q_mid = jnp.exp(bc - b_mid)
        e_k_mid = jnp.exp(b_mid - bc)
        e_q = jnp.exp(bc)
        e_k_end = jnp.exp(b_last - bc)
        dec = jnp.exp(b_last)
        qc = q_ref[rows, :] * (GLA_HK ** -0.5)
        kc = k_ref[rows, :]
        for h in range(GLA_HEADS):
            ks = slice(h * GLA_HK, (h + 1) * GLA_HK)
            vs = slice(h * GLA_HV, (h + 1) * GLA_HV)
            qm = (qc[:, ks] * e_q_mid[:, ks]).astype(BF16)
            km = (kc[:, ks] * e_k_mid[:, ks]).astype(BF16)
            s = lax.dot_general(qm, km, (((1,), (1,)), ((), ())), preferred_element_type=F32)
            s = jnp.where(causal, s, 0.0)
            vh = v_ref[rows, vs].astype(BF16)
            o = jnp.dot(s.astype(BF16), vh, preferred_element_type=F32)
            qd = (qc[:, ks] * e_q[:, ks]).astype(BF16)
            st = st_ref[h]
            o = o + lax.dot_general(qd, st.astype(BF16), (((1,), (1,)), ((), ())),
                                    preferred_element_type=F32)
            ke = (kc[:, ks] * e_k_end[:, ks]).astype(BF16)
            st_ref[h] = st * dec[:, ks] + lax.dot_general(
                vh, ke, (((0,), (0,)), ((), ())), preferred_element_type=F32)
            ms = jnp.mean(o * o, axis=-1, keepdims=True)
            y = o * lax.rsqrt(ms + LN_EPS) * ng_ref[:, vs]
            rr = r_ref[rows, vs]
            o_ref[rows, vs] = y * (rr * jax.nn.sigmoid(rr))
        return carry

    lax.fori_loop(0, n_chunks, chunk, 0)


def _gla_call(proj, wg_pad_bf, b_gate, norm_g, batch, seq):
    t = proj.shape[0]
    rows = min(GLA_ROWS, seq)
    per_seq = seq // rows
    tri = jnp.asarray(np.tril(np.ones((GLA_CHUNK, GLA_CHUNK), np.float32)))
    row_map = lambda b, j: b * per_seq + j
    return pl.pallas_call(
        _gla_kernel,
        grid=(batch, per_seq),
        in_specs=[
            pl.BlockSpec((rows, GLA_DK), lambda b, j: (row_map(b, j), 0)),
            pl.BlockSpec((rows, GLA_DK), lambda b, j: (row_map(b, j), 1)),
            pl.BlockSpec((rows, GLA_DV), lambda b, j: (row_map(b, j), 1)),
            pl.BlockSpec((rows, GLA_DV), lambda b, j: (row_map(b, j), 2)),
            pl.BlockSpec((rows, LANES), lambda b, j: (row_map(b, j), (2 * GLA_DK + 2 * GLA_DV) // LANES)),
            pl.BlockSpec((LANES, GLA_DK), lambda b, j: (0, 0)),
            pl.BlockSpec((1, GLA_DK), lambda b, j: (0, 0)),
            pl.BlockSpec((1, GLA_DV), lambda b, j: (0, 0)),
            pl.BlockSpec((GLA_CHUNK, GLA_CHUNK), lambda b, j: (0, 0)),
        ],
        out_specs=pl.BlockSpec((rows, GLA_DV), lambda b, j: (row_map(b, j), 0)),
        out_shape=jax.ShapeDtypeStruct((t, GLA_DV), F32),
        scratch_shapes=[
            pltpu.VMEM((rows, GLA_DK), F32),
            pltpu.VMEM((GLA_HEADS, GLA_HV, GLA_HK), F32),
        ],
        compiler_params=_cparams(("parallel", "arbitrary")),
        name="gla_core",
    )(proj, proj, proj, proj, proj, wg_pad_bf, b_gate.reshape(1, GLA_DK), norm_g.reshape(1, GLA_DV), tri)


def _rg_kernel(y_ref, x_ref, cw_ref, cb_ref, wg_ref, brg_ref, big_ref, lam_ref,
               o_ref, xext_ref, hcar_ref):
    rows = x_ref.shape[0]

    @pl.when(pl.program_id(1) == 0)
    def _():
        xext_ref[0:8, :] = jnp.zeros((8, RG_WIDTH), F32)
        hcar_ref[...] = jnp.zeros_like(hcar_ref)

    xext_ref[8:8 + rows, :] = x_ref[...]
    xc = cb_ref[...] + xext_ref[pl.ds(8 - (RG_CONV - 1), rows), :] * cw_ref[0:1, :]
    for j in range(1, RG_CONV):
        xc = xc + xext_ref[pl.ds(8 - (RG_CONV - 1) + j, rows), :] * cw_ref[j:j + 1, :]
    xext_ref[0:8, :] = x_ref[rows - 8:rows, :]

    r_parts, i_parts = [], []
    for n in range(RG_BLOCKS):
        cs = slice(n * RG_BW, (n + 1) * RG_BW)
        g = jnp.dot(xc[:, cs].astype(BF16), wg_ref[n], preferred_element_type=F32)
        r_parts.append(g[:, :RG_BW])
        i_parts.append(g[:, RG_BW:])
    r = jax.nn.sigmoid(jnp.concatenate(r_parts, axis=-1) + brg_ref[...])
    ig = jax.nn.sigmoid(jnp.concatenate(i_parts, axis=-1) + big_ref[...])

    log_a = -RG_C * r * _softplus(-lam_ref[...])
    a = jnp.exp(log_a)
    th = jnp.tanh(log_a)
    u = jnp.sqrt(-2.0 * th / (1.0 - th)) * (ig * xc)

    ridx = lax.broadcasted_iota(I32, (rows, 1), 0)
    acc_a, acc_h = a, u
    d = 1
    while d < rows:
        keep = ridx >= d
        sh_a = pltpu.roll(acc_a, d, axis=0)
        sh_h = pltpu.roll(acc_h, d, axis=0)
        acc_h = jnp.where(keep, acc_a * sh_h + acc_h, acc_h)
        acc_a = jnp.where(keep, acc_a * sh_a, acc_a)
        d *= 2
    h = acc_h + acc_a * hcar_ref[...]
    hcar_ref[...] = h[rows - 1:rows, :]

    yb = y_ref[...]
    gelu = 0.5 * yb * (1.0 + jnp.tanh(0.7978845608028654 * (yb + 0.044715 * (yb * yb * yb))))
    o_ref[...] = h * gelu


def _rg_call(proj, conv_w, conv_b, wgate_bf, b_rg, b_ig, lam, batch, seq):
    t = proj.shape[0]
    rows = min(RG_ROWS, seq)
    per_seq = seq // rows
    row_map = lambda b, j: b * per_seq + j
    vec = lambda: pl.BlockSpec((1, RG_WIDTH), lambda b, j: (0, 0))
    return pl.pallas_call(
        _rg_kernel,
        grid=(batch, per_seq),
        in_specs=[
            pl.BlockSpec((rows, RG_WIDTH), lambda b, j: (row_map(b, j), 0)),
            pl.BlockSpec((rows, RG_WIDTH), lambda b, j: (row_map(b, j), 1)),
            pl.BlockSpec((RG_CONV, RG_WIDTH), lambda b, j: (0, 0)),
            vec(),
            pl.BlockSpec((RG_BLOCKS, RG_BW, 2 * RG_BW), lambda b, j: (0, 0, 0)),
            vec(), vec(), vec(),
        ],
        out_specs=pl.BlockSpec((rows, RG_WIDTH), lambda b, j: (row_map(b, j), 0)),
        out_shape=jax.ShapeDtypeStruct((t, RG_WIDTH), F32),
        scratch_shapes=[
            pltpu.VMEM((rows + 8, RG_WIDTH), F32),
            pltpu.VMEM((1, RG_WIDTH), F32),
        ],
        compiler_params=_cparams(("parallel", "arbitrary")),
        name="rglru_core",
    )(proj, proj, conv_w, conv_b.reshape(1, RG_WIDTH), wgate_bf, b_rg.reshape(1, RG_WIDTH),
      b_ig.reshape(1, RG_WIDTH), lam.reshape(1, RG_WIDTH))


def _fox_bias_selectors():
    piece = lambda part, head: part * FOX_HEADS + head
    one = 3 * FOX_HEADS
    sel_q = np.zeros((LANES, FOX_PAIRS * LANES), np.float32)
    sel_ka = np.zeros((LANES, FOX_PAIRS * LANES), np.float32)
    sel_kb = np.zeros((LANES, FOX_PAIRS * LANES), np.float32)
    for p in range(FOX_PAIRS):
        base = p * LANES
        for side, head in enumerate((2 * p, 2 * p + 1)):
            sel_k = sel_ka if side == 0 else sel_kb
            for part in range(3):
                sel_q[piece(part, head), base + 6 * side + part] = 1.0
                sel_q[one, base + 6 * side + 3 + part] = 1.0
                sel_k[one, base + 6 * side + part] = 1.0
                sel_k[piece(part, head), base + 6 * side + 3 + part] = -1.0
    return sel_q, sel_ka, sel_kb


def _fox_prep_kernel(q_ref, k_ref, v_ref, f_ref, bf_ref, gq_ref, gk_ref, grp_ref, tri_ref,
                     sq_ref, ska_ref, skb_ref, qa_ref, ka_ref, kb_ref, vb_ref, dcar_ref):
    @pl.when(pl.program_id(1) == 0)
    def _():
        dcar_ref[...] = jnp.zeros_like(dcar_ref)

    lane = lax.broadcasted_iota(I32, (1, LANES), 1)
    log_f = -_softplus(-(f_ref[...] + bf_ref[...]))
    log_f = jnp.where(lane < FOX_HEADS, log_f, 0.0)
    dcum = jnp.dot(tri_ref[...], log_f, preferred_element_type=F32,
                   precision=lax.Precision.HIGHEST) + dcar_ref[...]
    dcar_ref[...] = dcum[dcum.shape[0] - 1:, :]

    d2 = dcum * LOG2E
    d_hi = d2.astype(BF16)
    rem = d2 - d_hi.astype(F32)
    d_mid = rem.astype(BF16)
    d_lo = (rem - d_mid.astype(F32)).astype(BF16)
    hi, mid, lo = (pltpu.roll(p.astype(F32), s, axis=1) for p, s in
                   ((d_hi, 0), (d_mid, FOX_HEADS), (d_lo, 2 * FOX_HEADS)))
    pieces = jnp.where(lane < FOX_HEADS, hi,
                       jnp.where(lane < 2 * FOX_HEADS, mid,
                                 jnp.where(lane < 3 * FOX_HEADS, lo,
                                           jnp.where(lane == 3 * FOX_HEADS, 1.0, 0.0)))).astype(BF16)
    bias_q = jnp.dot(pieces, sq_ref[...], preferred_element_type=F32).astype(BF16)
    bias_ka = jnp.dot(pieces, ska_ref[...], preferred_element_type=F32).astype(BF16)
    bias_kb = jnp.dot(pieces, skb_ref[...], preferred_element_type=F32).astype(BF16)

    grp = grp_ref[...]
    first_head = lane < FOX_HD

    def rms(xt, gain):
        sq = xt * xt
        sq_hi = sq.astype(BF16)
        sq_lo = (sq - sq_hi.astype(F32)).astype(BF16)
        ssum = (jnp.dot(sq_hi, grp, preferred_element_type=F32)
                + jnp.dot(sq_lo, grp, preferred_element_type=F32))
        return xt * lax.rsqrt(ssum * (1.0 / FOX_HD) + LN_EPS) * gain

    for p in range(FOX_PAIRS):
        src = slice(p * LANES, (p + 1) * LANES)
        dst = slice(p * FOX_AUG, p * FOX_AUG + LANES)
        dst_bias = slice(p * FOX_AUG + LANES, (p + 1) * FOX_AUG)
        qn = rms(q_ref[:, src], gq_ref[...] * (FOX_HD ** -0.5 * LOG2E)).astype(BF16)
        kn = rms(k_ref[:, src], gk_ref[...]).astype(BF16)
        qa_ref[:, dst] = qn
        qa_ref[:, dst_bias] = bias_q[:, src]
        ka_ref[:, dst] = jnp.where(first_head, kn, jnp.zeros_like(kn))
        ka_ref[:, dst_bias] = bias_ka[:, src]
        kb_ref[:, dst] = jnp.where(first_head, jnp.zeros_like(kn), kn)
        kb_ref[:, dst_bias] = bias_kb[:, src]
    vb_ref[...] = v_ref[...].astype(BF16)


def _fox_prep_call(proj, b_f, q_norm_g, k_norm_g, batch, seq):
    t = proj.shape[0]
    d = FOX_HEADS * FOX_HD
    rows = min(FOX_PREP_ROWS, seq)
    per_seq = seq // rows
    row_map = lambda b, j: b * per_seq + j
    sel_q, sel_ka, sel_kb = (jnp.asarray(s, BF16) for s in _fox_bias_selectors())
    grp = np.kron(np.eye(2, dtype=np.float32), np.ones((FOX_HD, FOX_HD), np.float32))
    tri = np.tril(np.ones((rows, rows), np.float32))
    bf_pad = jnp.zeros((1, LANES), F32).at[0, :FOX_HEADS].set(b_f)
    gq2 = jnp.tile(q_norm_g, 2).reshape(1, LANES)
    gk2 = jnp.tile(k_norm_g, 2).reshape(1, LANES)
    const = lambda shape: pl.BlockSpec(shape, lambda b, j: tuple(0 for _ in shape))
    aug = jax.ShapeDtypeStruct((t, FOX_PAIRS * FOX_AUG), BF16)
    aug_spec = pl.BlockSpec((rows, FOX_PAIRS * FOX_AUG), lambda b, j: (row_map(b, j), 0))
    return pl.pallas_call(
        _fox_prep_kernel,
        grid=(batch, per_seq),
        in_specs=[
            pl.BlockSpec((rows, d), lambda b, j: (row_map(b, j), 0)),
            pl.BlockSpec((rows, d), lambda b, j: (row_map(b, j), 1)),
            pl.BlockSpec((rows, d), lambda b, j: (row_map(b, j), 2)),
            pl.BlockSpec((rows, LANES), lambda b, j: (row_map(b, j), 4 * d // LANES)),
            const((1, LANES)), const((1, LANES)), const((1, LANES)),
            const((LANES, LANES)), const((rows, rows)),
            const((LANES, FOX_PAIRS * LANES)), const((LANES, FOX_PAIRS * LANES)),
            const((LANES, FOX_PAIRS * LANES)),
        ],
        out_specs=[aug_spec, aug_spec, aug_spec,
                   pl.BlockSpec((rows, d), lambda b, j: (row_map(b, j), 0))],
        out_shape=[aug, aug, aug, jax.ShapeDtypeStruct((t, d), BF16)],
        scratch_shapes=[pltpu.VMEM((1, LANES), F32)],
        compiler_params=_cparams(("parallel", "arbitrary")),
        name="fox_prep",
    )(proj, proj, proj, proj, bf_pad, gq2, gk2, jnp.asarray(grp, BF16), jnp.asarray(tri),
      sel_q, sel_ka, sel_kb)


def _fox_attn_kernel(qa_ref, ka_ref, kb_ref, v_ref, og_ref, o_ref, m_ref, l_ref, acc_ref):
    qi = pl.program_id(2)
    tq = qa_ref.shape[0]
    lane = lax.broadcasted_iota(I32, (1, LANES), 1)
    first_head = lane < FOX_HD

    m_ref[...] = jnp.full_like(m_ref, NEG_BIG)
    l_ref[...] = jnp.zeros_like(l_ref)
    acc_ref[...] = jnp.zeros_like(acc_ref)

    def step(ki, masked):
        rows = pl.ds(pl.multiple_of(ki * tq, tq), tq)
        q = qa_ref[...]
        v = v_ref[rows, :]
        pv, alpha = [], []
        for side, k_ref in enumerate((ka_ref, kb_ref)):
            s = lax.dot_general(q, k_ref[rows, :], (((1,), (1,)), ((), ())), preferred_element_type=F32)
            if masked:
                row = lax.broadcasted_iota(I32, s.shape, 0)
                col = lax.broadcasted_iota(I32, s.shape, 1)
                s = jnp.where(col <= row, s, NEG_BIG)
            m_prev = m_ref[side]
            m_new = jnp.maximum(m_prev, jnp.max(s, axis=-1, keepdims=True))
            a = jnp.exp2(m_prev - m_new)
            p = jnp.exp2(s - m_new)
            l_ref[side] = a * l_ref[side] + jnp.sum(p, axis=-1, keepdims=True)
            m_ref[side] = m_new
            pv.append(jnp.dot(p.astype(BF16), v, preferred_element_type=F32))
            alpha.append(a)
        acc_ref[...] = (acc_ref[...] * jnp.where(first_head, alpha[0], alpha[1])
                        + jnp.where(first_head, pv[0], pv[1]))

    def body(ki, carry):
        step(ki, False)
        return carry

    lax.fori_loop(0, qi, body, 0)
    step(qi, True)
    og = og_ref[...]
    o_ref[...] = acc_ref[...] / jnp.where(first_head, l_ref[0], l_ref[1]) * jax.nn.sigmoid(og)


def _fox_attn_call(qa, ka, kb, vb, proj, batch, seq):
    t = qa.shape[0]
    d = FOX_HEADS * FOX_HD
    tq = min(FOX_TQ, seq)
    nq = seq // tq
    q_map = lambda b, p, i: (b * nq + i, p)
    kv_map = lambda b, p, i: (b, p)
    return pl.pallas_call(
        _fox_attn_kernel,
        grid=(batch, FOX_PAIRS, nq),
        in_specs=[
            pl.BlockSpec((tq, FOX_AUG), q_map),
            pl.BlockSpec((seq, FOX_AUG), kv_map),
            pl.BlockSpec((seq, FOX_AUG), kv_map),
            pl.BlockSpec((seq, LANES), kv_map),
            pl.BlockSpec((tq, LANES), lambda b, p, i: (b * nq + i, 3 * d // LANES + p)),
        ],
        out_specs=pl.BlockSpec((tq, LANES), q_map),
        out_shape=jax.ShapeDtypeStruct((t, d), F32),
        scratch_shapes=[
            pltpu.VMEM((2, tq, 1), F32),
            pltpu.VMEM((2, tq, 1), F32),
            pltpu.VMEM((tq, LANES), F32),
        ],
        compiler_params=_cparams(("parallel", "parallel", "arbitrary")),
        name="fox_attn",
    )(qa, ka, kb, vb, proj)


def _router_kernel(x_ref, sc_ref, sh_ref, whi_ref, wlo_ref, b_ref, h_ref, idx_ref, wt_ref):
    h = x_ref[...] * (1.0 + sc_ref[0]) + sh_ref[0]
    h_ref[...] = h
    h_hi = h.astype(BF16)
    h_lo = (h - h_hi.astype(F32)).astype(BF16)
    logits = (jnp.dot(h_hi, whi_ref[...], preferred_element_type=F32)
              + jnp.dot(h_lo, whi_ref[...], preferred_element_type=F32)
              + jnp.dot(h_hi, wlo_ref[...], preferred_element_type=F32)
              + b_ref[...])
    lane = lax.broadcasted_iota(I32, logits.shape, 1)
    logits = jnp.where(lane < N_EXPERTS, logits, NEG_BIG)
    idx_out = jnp.zeros(logits.shape, I32)
    wt_out = jnp.zeros(logits.shape, F32)
    top0 = None
    denom = None
    for k in range(TOP_K):
        m = jnp.max(logits, axis=-1, keepdims=True)
        sel = jnp.min(jnp.where(logits == m, lane, LANES), axis=-1, keepdims=True)
        logits = jnp.where(lane == sel, NEG_BIG, logits)
        if k == 0:
            top0 = m
        e = jnp.exp(m - top0)
        denom = e if k == 0 else denom + e
        idx_out = jnp.where(lane == k, sel, idx_out)
        wt_out = jnp.where(lane == k, e, wt_out)
    idx_ref[...] = idx_out
    wt_ref[...] = wt_out / denom


def _router_call(x, sc, sh, w_hi, w_lo, b_pad, seq, tm):
    t, d = x.shape
    per_seq = seq // tm
    const = lambda shape: pl.BlockSpec(shape, lambda i: tuple(0 for _ in shape))
    return pl.pallas_call(
        _router_kernel,
        grid=(t // tm,),
        in_specs=[
            pl.BlockSpec((tm, d), lambda i: (i, 0)),
            pl.BlockSpec((1, 1, d), lambda i: (i // per_seq, 0, 0)),
            pl.BlockSpec((1, 1, d), lambda i: (i // per_seq, 0, 0)),
            const((d, LANES)), const((d, LANES)), const((1, LANES)),
        ],
        out_specs=[
            pl.BlockSpec((tm, d), lambda i: (i, 0)),
            pl.BlockSpec((tm, LANES), lambda i: (i, 0)),
            pl.BlockSpec((tm, LANES), lambda i: (i, 0)),
        ],
        out_shape=[
            jax.ShapeDtypeStruct((t, d), F32),
            jax.ShapeDtypeStruct((t, LANES), I32),
            jax.ShapeDtypeStruct((t, LANES), F32),
        ],
        compiler_params=_cparams(("parallel",)),
        name="moe_router",
    )(x, sc, sh, w_hi, w_lo, b_pad)


MOE_GROUP = 2 * LANES


def _moe_kernel(be_ref, src_ref, dst_ref, nact_ref,
                h_hbm, wgu_ref, bgu_ref, wdn_ref, bdn_ref, perm_ref,
                y_hbm,
                xa, xb, ya, yb, wgu_bf, wdn_bf, gsem, ssem):
    i = pl.program_id(0)
    n_act = nact_ref[0]
    rows = MOE_ROWS
    n_groups = wgu_ref.shape[3] // MOE_GROUP
    xbufs, ybufs = (xa, xb), (ya, yb)

    def gather_start(blk, r, par):
        tok = src_ref[blk * rows + r]
        pltpu.make_async_copy(h_hbm.at[tok], xbufs[par].at[pl.ds(r, 1), :], gsem.at[par]).start()

    def scatter_start(blk_ext, r, par):
        row = dst_ref[blk_ext * rows + r]
        pltpu.make_async_copy(ybufs[par].at[pl.ds(r, 1), :], y_hbm.at[pl.ds(row, 1), :], ssem.at[par]).start()

    def wait_gather(par):
        pltpu.make_async_copy(xbufs[par], xbufs[par], gsem.at[par]).wait()

    def wait_scatter(par):
        pltpu.make_async_copy(ybufs[par], ybufs[par], ssem.at[par]).wait()

    @pl.when(i == 0)
    def _():
        def body(r, c):
            gather_start(0, r, 0)
            return c
        lax.fori_loop(0, rows, body, 0)
        for par in range(2):
            ybufs[par][...] = jnp.zeros_like(ybufs[par])
            zero_dump = pltpu.make_async_copy(
                ybufs[par], y_hbm.at[pl.ds(y_hbm.shape[0] - (2 - par) * rows, rows), :], ssem.at[par])
            zero_dump.start()
            zero_dump.wait()

    def block_step(cur):
        nxt = 1 - cur
        wait_gather(cur)
        for r in range(rows):
            gather_start(i + 1, r, nxt)
        for r in range(rows):
            scatter_start(i, r, nxt)
        x = xbufs[cur][...].astype(BF16)
        gu = jnp.dot(x, wgu_bf[...], preferred_element_type=F32) + bgu_ref[0, 0]
        parts = []
        for g in range(n_groups):
            gate = jnp.minimum(gu[:, g * MOE_GROUP:g * MOE_GROUP + LANES], SWIGLU_LIMIT)
            up = jnp.clip(gu[:, g * MOE_GROUP + LANES:(g + 1) * MOE_GROUP], -SWIGLU_LIMIT, SWIGLU_LIMIT)
            glu = gate * jax.nn.sigmoid(SWIGLU_ALPHA * gate)
            parts.append(((up + 1.0) * glu).astype(BF16))
        hmid = jnp.concatenate(parts, axis=-1)
        ybufs[cur][...] = jnp.dot(hmid, wdn_bf[...], preferred_element_type=F32) + bdn_ref[0, 0]
        wait_scatter(nxt)

        @pl.when(i == n_act - 1)
        def _():
            def body(r, c):
                scatter_start(i + 1, r, cur)
                return c
            lax.fori_loop(0, rows, body, 0)
            wait_scatter(cur)
            wait_gather(nxt)

    @pl.when(i < n_act)
    def _():
        e_changed = jnp.logical_or(i == 0, be_ref[i] != be_ref[jnp.maximum(i - 1, 0)])

        @pl.when(e_changed)
        def _():
            perm = perm_ref[...]
            for g in range(n_groups):
                cs = slice(g * MOE_GROUP, (g + 1) * MOE_GROUP)
                wgu_bf[:, cs] = jnp.dot(wgu_ref[0, 0, :, cs].astype(BF16), perm,
                                        preferred_element_type=F32).astype(BF16)
            wdn_bf[...] = wdn_ref[0, 0].astype(BF16)

        for par in range(2):
            @pl.when(i % 2 == par)
            def _(par=par):
                block_step(par)


def _moe_call(h, be, src, dst, nact, w_gu, b_gu, w_down, b_down, layer, n_blocks, n_out_rows):
    d = h.shape[-1]
    _, n_exp, _, f2 = w_gu.shape
    f = w_down.shape[2]
    perm = np.zeros((MOE_GROUP, MOE_GROUP), np.float32)
    for jj in range(LANES):
        perm[2 * jj, jj] = 1.0
        perm[2 * jj + 1, LANES + jj] = 1.0
    b_gu_grouped = b_gu[layer].reshape(n_exp, f2 // MOE_GROUP, LANES, 2).transpose(0, 1, 3, 2).reshape(
        1, n_exp, 1, f2)
    by_expert = lambda i, be, s, ds_, na: (layer, be[i], 0, 0)
    grid_spec = pltpu.PrefetchScalarGridSpec(
        num_scalar_prefetch=4,
        grid=(n_blocks,),
        in_specs=[
            pl.BlockSpec(memory_space=pl.ANY),
            pl.BlockSpec((1, 1, d, f2), by_expert),
            pl.BlockSpec((1, 1, 1, f2), lambda i, be, s, ds_, na: (0, be[i], 0, 0)),
            pl.BlockSpec((1, 1, f, d), by_expert),
            pl.BlockSpec((1, 1, 1, d), by_expert),
            pl.BlockSpec((MOE_GROUP, MOE_GROUP), lambda i, be, s, ds_, na: (0, 0)),
        ],
        out_specs=pl.BlockSpec(memory_space=pl.ANY),
        scratch_shapes=[
            pltpu.VMEM((MOE_ROWS, d), F32),
            pltpu.VMEM((MOE_ROWS, d), F32),
            pltpu.VMEM((MOE_ROWS, d), F32),
            pltpu.VMEM((MOE_ROWS, d), F32),
            pltpu.VMEM((d, f2), BF16),
            pltpu.VMEM((f, d), BF16),
            pltpu.SemaphoreType.DMA((2,)),
            pltpu.SemaphoreType.DMA((2,)),
        ],
    )
    return pl.pallas_call(
        _moe_kernel,
        grid_spec=grid_spec,
        out_shape=jax.ShapeDtypeStruct((n_out_rows, d), F32),
        compiler_params=_cparams(("arbitrary",)),
        name="moe_experts",
    )(be, src, dst, nact, h, w_gu, b_gu_grouped, w_down, b_down.reshape(b_down.shape[0], n_exp, 1, d),
      jnp.asarray(perm, BF16))


def _combine_kernel(y0_ref, y1_ref, y2_ref, y3_ref, wt_ref, x_ref, g_ref, lg_ref, lb_ref, o_ref):
    wt = wt_ref[...]
    y = y0_ref[...] * wt[:, 0:1]
    for k, y_ref in enumerate((y1_ref, y2_ref, y3_ref), start=1):
        y = y + y_ref[...] * wt[:, k:k + 1]
    z = ALPHA * x_ref[...] + (1.0 + g_ref[0]) * y
    o_ref[...] = _layer_norm_rows(z, lg_ref[...], lb_ref[...])


def _combine_call(y4, wt, x, gate, ln_g, ln_b, seq, tm):
    t, d = x.shape
    per_seq = seq // tm
    nblk = t // tm
    y_spec = lambda k: pl.BlockSpec((tm, d), lambda i, k=k: (k * nblk + i, 0))
    return pl.pallas_call(
        _combine_kernel,
        grid=(nblk,),
        in_specs=[
            y_spec(0), y_spec(1), y_spec(2), y_spec(3),
            pl.BlockSpec((tm, LANES), lambda i: (i, 0)),
            pl.BlockSpec((tm, d), lambda i: (i, 0)),
            pl.BlockSpec((1, 1, d), lambda i: (i // per_seq, 0, 0)),
            pl.BlockSpec((1, d), lambda i: (0, 0)),
            pl.BlockSpec((1, d), lambda i: (0, 0)),
        ],
        out_specs=pl.BlockSpec((tm, d), lambda i: (i, 0)),
        out_shape=jax.ShapeDtypeStruct((t, d), F32),
        compiler_params=_cparams(("parallel",)),
        name="moe_combine_ln",
    )(y4, y4, y4, y4, wt, x, gate, ln_g.reshape(1, d), ln_b.reshape(1, d))


def _moe_plan(top_idx, t):
    n_assign = t * TOP_K
    n_blocks = -(-(n_assign + N_EXPERTS * (MOE_ROWS - 1)) // MOE_ROWS)
    n_pad = n_blocks * MOE_ROWS
    experts = jnp.arange(N_EXPERTS, dtype=I32)
    flat_e = top_idx.reshape(-1)
    order = jnp.argsort(flat_e).astype(I32)
    counts = jnp.sum((flat_e[:, None] == experts[None, :]).astype(I32), axis=0)
    starts = jnp.cumsum(counts) - counts
    padded = (counts + MOE_ROWS - 1) // MOE_ROWS * MOE_ROWS
    pad_ends = jnp.cumsum(padded)
    pad_starts = pad_ends - padded
    first_row = jnp.arange(n_blocks + 1, dtype=I32) * MOE_ROWS
    block_e = jnp.minimum(jnp.sum((pad_ends[None, :] <= first_row[:, None]).astype(I32), axis=1), N_EXPERTS - 1)
    pos = jnp.arange(n_pad + MOE_ROWS, dtype=I32)
    e_pos = jnp.repeat(block_e, MOE_ROWS)
    off = pos - pad_starts[e_pos]
    valid = jnp.logical_and(off < counts[e_pos], pos < pad_ends[-1])
    assign = order[jnp.clip(starts[e_pos] + off, 0, n_assign - 1)]
    dump = n_assign + ((pos // MOE_ROWS) % 2) * MOE_ROWS + pos % MOE_ROWS
    src = jnp.where(valid, assign // TOP_K, 0)
    dst = jnp.where(valid, (assign % TOP_K) * t + assign // TOP_K, dump)
    dst_ext = jnp.concatenate([dump[MOE_ROWS:2 * MOE_ROWS], dst[:n_pad]])
    n_act = (pad_ends[-1] // MOE_ROWS).astype(I32).reshape(1)
    return block_e[:n_blocks], src, dst_ext, n_act, n_blocks, n_assign + 2 * MOE_ROWS


def _moe_layer(x, sc, sh, gate, ln_g, ln_b, w_router, b_router, w_gu, b_gu, w_down, b_down, layer, seq, tm):
    t, d = x.shape
    w_pad = jnp.zeros((d, LANES), F32).at[:, :N_EXPERTS].set(w_router)
    w_hi = w_pad.astype(BF16)
    w_lo = (w_pad - w_hi.astype(F32)).astype(BF16)
    b_pad = jnp.zeros((1, LANES), F32).at[0, :N_EXPERTS].set(b_router)
    h, idx, wt = _router_call(x, sc, sh, w_hi, w_lo, b_pad, seq, tm)
    block_e, src, dst, n_act, n_blocks, n_rows = _moe_plan(idx[:, :TOP_K], t)
    y4 = _moe_call(h.reshape(t, 1, d), block_e, src, dst, n_act, w_gu, b_gu, w_down, b_down, layer,
                   n_blocks, n_rows)
    return _combine_call(y4, wt, x, gate, ln_g, ln_b, seq, tm)


def _pad_cols(w, n):
    return jnp.pad(w, ((0, 0), (0, n - w.shape[1])))


def kernel(x, c, ada_w, ada_b, ln_g, ln_b, gla_w_in, gla_w_gate_up, gla_b_gate, gla_norm_g, gla_w_out, rg_w_in, rg_conv_w, rg_conv_b, rg_w_rg, rg_b_rg, rg_w_ig, rg_b_ig, rg_lambda, rg_w_out, fox_w_in, fox_b_f, fox_q_norm_g, fox_k_norm_g, fox_w_out, moe_w_router, moe_b_router, moe_w_gu, moe_b_gu, moe_w_down, moe_b_down):
    batch, seq, d = x.shape
    t = batch * seq
    tm = min(512, seq)
    depth = ada_w.shape[0]
    xt = x.reshape(t, d)
    mod = _ada_call(c, ada_w, ada_b)

    for l in range(depth):
        part = lambda i: mod[l, :, i * d:(i + 1) * d].reshape(batch, 1, d)
        sh1, sc1, g1, sh2, sc2, g2 = (part(i) for i in range(6))
        kind, j = l % 3, l // 3
        if kind == 0:
            n_in = 2 * GLA_DK + 2 * GLA_DV + LANES
            proj = _modmm_call(xt, sc1, sh1, _pad_cols(gla_w_in[j], n_in).astype(BF16), seq, tm)
            wg = jnp.pad(gla_w_gate_up[j], ((0, LANES - GLA_RANK), (0, 0))).astype(BF16)
            y = _gla_call(proj, wg, gla_b_gate[j], gla_norm_g[j], batch, seq)
            w_out = gla_w_out[j]
        elif kind == 1:
            proj = _modmm_call(xt, sc1, sh1, rg_w_in[j].astype(BF16), seq, tm)
            wgate = jnp.concatenate([rg_w_rg[j], rg_w_ig[j]], axis=-1).astype(BF16)
            y = _rg_call(proj, rg_conv_w[j], rg_conv_b[j], wgate, rg_b_rg[j], rg_b_ig[j],
                         rg_lambda[j], batch, seq)
            w_out = rg_w_out[j]
        else:
            w = fox_w_in[j]
            w = jnp.concatenate([w[:, :3 * d], w[:, 3 * d + FOX_HEADS:], w[:, 3 * d:3 * d + FOX_HEADS]], axis=1)
            proj = _modmm_call(xt, sc1, sh1, _pad_cols(w, 4 * d + LANES).astype(BF16), seq, tm)
            qa, ka, kb, vb = _fox_prep_call(proj, fox_b_f[j], fox_q_norm_g[j], fox_k_norm_g[j], batch, seq)
            y = _fox_attn_call(qa, ka, kb, vb, proj, batch, seq)
            w_out = fox_w_out[j]
        xt = _outln_call(y, w_out.astype(BF16), xt, g1, ln_g[l, 0], ln_b[l, 0], seq, tm)
        xt = _moe_layer(xt, sc2, sh2, g2, ln_g[l, 1], ln_b[l, 1], moe_w_router[l], moe_b_router[l],
                        moe_w_gu, moe_b_gu, moe_w_down, moe_b_down, l, seq, tm)
    return xt.reshape(batch, seq, d)
```

```python
import functools

import numpy as np
import jax
import jax.numpy as jnp
from jax import lax
from jax.experimental import pallas as pl
from jax.experimental.pallas import tpu as pltpu

F32 = jnp.float32
BF16 = jnp.bfloat16
I32 = jnp.int32

DEPTH = 4
ALPHA = (2.0 * DEPTH) ** 0.25
LN_EPS = 1e-5

GLA_HEADS = 4
GLA_HK = 128
GLA_HV = 256
GLA_DK = GLA_HEADS * GLA_HK
GLA_DV = GLA_HEADS * GLA_HV
GLA_RANK = 16
GLA_TAU = 16.0
GLA_CHUNK = 64
GLA_ROWS = 512

RG_BW = 128
RG_BLOCKS = 10
RG_WIDTH = RG_BW * RG_BLOCKS
RG_CONV = 4
RG_C = 8.0
RG_ROWS = 256

FOX_HD = 64
FOX_HEADS = 16
FOX_PAIRS = FOX_HEADS // 2
FOX_TQ = 512
FOX_PREP_ROWS = 256
FOX_AUG = 256

N_EXPERTS = 32
TOP_K = 4
SWIGLU_LIMIT = 7.0
SWIGLU_ALPHA = 1.702
MOE_ROWS = 256
NEG_BIG = -1e30
LOG2E = 1.4426950408889634

LANES = 128
VMEM_LIMIT = 56 * 1024 * 1024


def _cparams(sem, vmem=VMEM_LIMIT):
    return pltpu.CompilerParams(dimension_semantics=sem, vmem_limit_bytes=vmem)


def _softplus(z):
    return jnp.maximum(z, 0.0) + jnp.log(1.0 + jnp.exp(-jnp.abs(z)))


def _layer_norm_rows(z, g, b):
    mu = jnp.mean(z, axis=-1, keepdims=True)
    zc = z - mu
    var = jnp.mean(zc * zc, axis=-1, keepdims=True)
    return zc * lax.rsqrt(var + LN_EPS) * g + b


def _ada_kernel(c_ref, w_ref, b_ref, o_ref):
    c = c_ref[...]
    ca = c * jax.nn.sigmoid(c)
    o_ref[0] = jnp.dot(ca.astype(BF16), w_ref[0].astype(BF16), preferred_element_type=F32) + b_ref[0]


def _ada_call(c, ada_w, ada_b):
    depth, d, n = ada_w.shape
    b = c.shape[0]
    tn = n // 4
    return pl.pallas_call(
        _ada_kernel,
        grid=(depth, n // tn),
        in_specs=[
            pl.BlockSpec((b, d), lambda l, j: (0, 0)),
            pl.BlockSpec((1, d, tn), lambda l, j: (l, 0, j)),
            pl.BlockSpec((1, 1, tn), lambda l, j: (l, 0, j)),
        ],
        out_specs=pl.BlockSpec((1, b, tn), lambda l, j: (l, 0, j)),
        out_shape=jax.ShapeDtypeStruct((depth, b, n), F32),
        compiler_params=_cparams(("parallel", "parallel")),
        name="ada_mod",
    )(c, ada_w, ada_b.reshape(depth, 1, n))


def _modmm_kernel(x_ref, sc_ref, sh_ref, w_ref, o_ref):
    h = x_ref[...] * (1.0 + sc_ref[0]) + sh_ref[0]
    o_ref[...] = jnp.dot(h.astype(BF16), w_ref[...], preferred_element_type=F32)


def _modmm_call(x, sc, sh, w_bf, seq, tm):
    t, d = x.shape
    n = w_bf.shape[1]
    per_seq = seq // tm
    return pl.pallas_call(
        _modmm_kernel,
        grid=(t // tm,),
        in_specs=[
            pl.BlockSpec((tm, d), lambda i: (i, 0)),
            pl.BlockSpec((1, 1, d), lambda i: (i // per_seq, 0, 0)),
            pl.BlockSpec((1, 1, d), lambda i: (i // per_seq, 0, 0)),
            pl.BlockSpec((d, n), lambda i: (0, 0)),
        ],
        out_specs=pl.BlockSpec((tm, n), lambda i: (i, 0)),
        out_shape=jax.ShapeDtypeStruct((t, n), F32),
        compiler_params=_cparams(("parallel",)),
        name="mod_inproj",
    )(x, sc, sh, w_bf)


def _outln_kernel(y_ref, w_ref, x_ref, g_ref, lg_ref, lb_ref, o_ref):
    y = jnp.dot(y_ref[...].astype(BF16), w_ref[...], preferred_element_type=F32)
    z = ALPHA * x_ref[...] + (1.0 + g_ref[0]) * y
    o_ref[...] = _layer_norm_rows(z, lg_ref[...], lb_ref[...])


def _outln_call(y, w_bf, x, gate, ln_g, ln_b, seq, tm):
    t, k = y.shape
    d = x.shape[1]
    per_seq = seq // tm
    return pl.pallas_call(
        _outln_kernel,
        grid=(t // tm,),
        in_specs=[
            pl.BlockSpec((tm, k), lambda i: (i, 0)),
            pl.BlockSpec((k, d), lambda i: (0, 0)),
            pl.BlockSpec((tm, d), lambda i: (i, 0)),
            pl.BlockSpec((1, 1, d), lambda i: (i // per_seq, 0, 0)),
            pl.BlockSpec((1, d), lambda i: (0, 0)),
            pl.BlockSpec((1, d), lambda i: (0, 0)),
        ],
        out_specs=pl.BlockSpec((tm, d), lambda i: (i, 0)),
        out_shape=jax.ShapeDtypeStruct((t, d), F32),
        compiler_params=_cparams(("parallel",)),
        name="outproj_ln",
    )(y, w_bf, x, gate, ln_g.reshape(1, d), ln_b.reshape(1, d))


def _gla_kernel(q_ref, k_ref, v_ref, r_ref, g_ref, wg_ref, bg_ref, ng_ref, tri_ref,
                o_ref, la_ref, st_ref):
    c_rows = GLA_CHUNK
    half = c_rows // 2

    @pl.when(pl.program_id(1) == 0)
    def _():
        st_ref[...] = jnp.zeros_like(st_ref)

    z = jnp.dot(g_ref[...].astype(BF16), wg_ref[...], preferred_element_type=F32) + bg_ref[...]
    la_ref[...] = -_softplus(-z) * (1.0 / GLA_TAU)

    row = lax.broadcasted_iota(I32, (c_rows, c_rows), 0)
    col = lax.broadcasted_iota(I32, (c_rows, c_rows), 1)
    causal = col <= row
    tri = tri_ref[...]
    n_chunks = q_ref.shape[0] // c_rows

    def chunk(c, carry):
        r0 = pl.multiple_of(c * c_rows, c_rows)
        rows = pl.ds(r0, c_rows)
        bc = jnp.dot(tri, la_ref[rows, :], preferred_element_type=F32, precision=lax.Precision.HIGHEST)
        b_mid = bc[half - 1:half, :]
        b_last = bc[c_rows - 1:c_rows, :]
        e_q_mid = jnp.exp(bc - b_mid)
        e_k_mid = jnp.exp(b_mid - bc)
        e_q = jnp.exp(bc)
        e_k_end = jnp.exp(b_last - bc)
        dec = jnp.exp(b_last)
        qc = q_ref[rows, :] * (GLA_HK ** -0.5)
        kc = k_ref[rows, :]
        for h in range(GLA_HEADS):
            ks = slice(h * GLA_HK, (h + 1) * GLA_HK)
            vs = slice(h * GLA_HV, (h + 1) * GLA_HV)
            qm = (qc[:, ks] * e_q_mid[:, ks]).astype(BF16)
            km = (kc[:, ks] * e_k_mid[:, ks]).astype(BF16)
            s = lax.dot_general(qm, km, (((1,), (1,)), ((), ())), preferred_element_type=F32)
            s = jnp.where(causal, s, 0.0)
            vh = v_ref[rows, vs].astype(BF16)
            o = jnp.dot(s.astype(BF16), vh, preferred_element_type=F32)
            qd = (qc[:, ks] * e_q[:, ks]).astype(BF16)
            st = st_ref[h]
            o = o + lax.dot_general(qd, st.astype(BF16), (((1,), (1,)), ((), ())),
                                    preferred_element_type=F32)
            ke = (kc[:, ks] * e_k_end[:, ks]).astype(BF16)
            st_ref[h] = st * dec[:, ks] + lax.dot_general(
                vh, ke, (((0,), (0,)), ((), ())), preferred_element_type=F32)
            ms = jnp.mean(o * o, axis=-1, keepdims=True)
            y = o * lax.rsqrt(ms + LN_EPS) * ng_ref[:, vs]
            rr = r_ref[rows, vs]
            o_ref[rows, vs] = y * (rr * jax.nn.sigmoid(rr))
        return carry

    lax.fori_loop(0, n_chunks, chunk, 0)


def _gla_call(proj, wg_pad_bf, b_gate, norm_g, batch, seq):
    t = proj.shape[0]
    rows = min(GLA_ROWS, seq)
    per_seq = seq // rows
    tri = jnp.asarray(np.tril(np.ones((GLA_CHUNK, GLA_CHUNK), np.float32)))
    row_map = lambda b, j: b * per_seq + j
    return pl.pallas_call(
        _gla_kernel,
        grid=(batch, per_seq),
        in_specs=[
            pl.BlockSpec((rows, GLA_DK), lambda b, j: (row_map(b, j), 0)),
            pl.BlockSpec((rows, GLA_DK), lambda b, j: (row_map(b, j), 1)),
            pl.BlockSpec((rows, GLA_DV), lambda b, j: (row_map(b, j), 1)),
            pl.BlockSpec((rows, GLA_DV), lambda b, j: (row_map(b, j), 2)),
            pl.BlockSpec((rows, LANES), lambda b, j: (row_map(b, j), (2 * GLA_DK + 2 * GLA_DV) // LANES)),
            pl.BlockSpec((LANES, GLA_DK), lambda b, j: (0, 0)),
            pl.BlockSpec((1, GLA_DK), lambda b, j: (0, 0)),
            pl.BlockSpec((1, GLA_DV), lambda b, j: (0, 0)),
            pl.BlockSpec((GLA_CHUNK, GLA_CHUNK), lambda b, j: (0, 0)),
        ],
        out_specs=pl.BlockSpec((rows, GLA_DV), lambda b, j: (row_map(b, j), 0)),
        out_shape=jax.ShapeDtypeStruct((t, GLA_DV), F32),
        scratch_shapes=[
            pltpu.VMEM((rows, GLA_DK), F32),
            pltpu.VMEM((GLA_HEADS, GLA_HV, GLA_HK), F32),
        ],
        compiler_params=_cparams(("parallel", "arbitrary")),
        name="gla_core",
    )(proj, proj, proj, proj, proj, wg_pad_bf, b_gate.reshape(1, GLA_DK), norm_g.reshape(1, GLA_DV), tri)


def _rg_kernel(y_ref, x_ref, cw_ref, cb_ref, wg_ref, brg_ref, big_ref, lam_ref,
               o_ref, xext_ref, hcar_ref):
    rows = x_ref.shape[0]

    @pl.when(pl.program_id(1) == 0)
    def _():
        xext_ref[0:8, :] = jnp.zeros((8, RG_WIDTH), F32)
        hcar_ref[...] = jnp.zeros_like(hcar_ref)

    xext_ref[8:8 + rows, :] = x_ref[...]
    xc = cb_ref[...] + xext_ref[pl.ds(8 - (RG_CONV - 1), rows), :] * cw_ref[0:1, :]
    for j in range(1, RG_CONV):
        xc = xc + xext_ref[pl.ds(8 - (RG_CONV - 1) + j, rows), :] * cw_ref[j:j + 1, :]
    xext_ref[0:8, :] = x_ref[rows - 8:rows, :]

    r_parts, i_parts = [], []
    for n in range(RG_BLOCKS):
        cs = slice(n * RG_BW, (n + 1) * RG_BW)
        g = jnp.dot(xc[:, cs].astype(BF16), wg_ref[n], preferred_element_type=F32)
        r_parts.append(g[:, :RG_BW])
        i_parts.append(g[:, RG_BW:])
    r = jax.nn.sigmoid(jnp.concatenate(r_parts, axis=-1) + brg_ref[...])
    ig = jax.nn.sigmoid(jnp.concatenate(i_parts, axis=-1) + big_ref[...])

    log_a = -RG_C * r * _softplus(-lam_ref[...])
    a = jnp.exp(log_a)
    th = jnp.tanh(log_a)
    u = jnp.sqrt(-2.0 * th / (1.0 - th)) * (ig * xc)

    ridx = lax.broadcasted_iota(I32, (rows, 1), 0)
    acc_a, acc_h = a, u
    d = 1
    while d < rows:
        keep = ridx >= d
        sh_a = pltpu.roll(acc_a, d, axis=0)
        sh_h = pltpu.roll(acc_h, d, axis=0)
        acc_h = jnp.where(keep, acc_a * sh_h + acc_h, acc_h)
        acc_a = jnp.where(keep, acc_a * sh_a, acc_a)
        d *= 2
    h = acc_h + acc_a * hcar_ref[...]
    hcar_ref[...] = h[rows - 1:rows, :]

    yb = y_ref[...]
    gelu = 0.5 * yb * (1.0 + jnp.tanh(0.7978845608028654 * (yb + 0.044715 * (yb * yb * yb))))
    o_ref[...] = h * gelu


def _rg_call(proj, conv_w, conv_b, wgate_bf, b_rg, b_ig, lam, batch, seq):
    t = proj.shape[0]
    rows = min(RG_ROWS, seq)
    per_seq = seq // rows
    row_map = lambda b, j: b * per_seq + j
    vec = lambda: pl.BlockSpec((1, RG_WIDTH), lambda b, j: (0, 0))
    return pl.pallas_call(
        _rg_kernel,
        grid=(batch, per_seq),
        in_specs=[
            pl.BlockSpec((rows, RG_WIDTH), lambda b, j: (row_map(b, j), 0)),
            pl.BlockSpec((rows, RG_WIDTH), lambda b, j: (row_map(b, j), 1)),
            pl.BlockSpec((RG_CONV, RG_WIDTH), lambda b, j: (0, 0)),
            vec(),
            pl.BlockSpec((RG_BLOCKS, RG_BW, 2 * RG_BW), lambda b, j: (0, 0, 0)),
            vec(), vec(), vec(),
        ],
        out_specs=pl.BlockSpec((rows, RG_WIDTH), lambda b, j: (row_map(b, j), 0)),
        out_shape=jax.ShapeDtypeStruct((t, RG_WIDTH), F32),
        scratch_shapes=[
            pltpu.VMEM((rows + 8, RG_WIDTH), F32),
            pltpu.VMEM((1, RG_WIDTH), F32),
        ],
        compiler_params=_cparams(("parallel", "arbitrary")),
        name="rglru_core",
    )(proj, proj, conv_w, conv_b.reshape(1, RG_WIDTH), wgate_bf, b_rg.reshape(1, RG_WIDTH),
      b_ig.reshape(1, RG_WIDTH), lam.reshape(1, RG_WIDTH))


def _fox_bias_selectors():
    piece = lambda part, head: part * FOX_HEADS + head
    one = 3 * FOX_HEADS
    sel_q = np.zeros((LANES, FOX_PAIRS * LANES), np.float32)
    sel_ka = np.zeros((LANES, FOX_PAIRS * LANES), np.float32)
    sel_kb = np.zeros((LANES, FOX_PAIRS * LANES), np.float32)
    for p in range(FOX_PAIRS):
        base = p * LANES
        for side, head in enumerate((2 * p, 2 * p + 1)):
            sel_k = sel_ka if side == 0 else sel_kb
            for part in range(3):
                sel_q[piece(part, head), base + 6 * side + part] = 1.0
                sel_q[one, base + 6 * side + 3 + part] = 1.0
                sel_k[one, base + 6 * side + part] = 1.0
                sel_k[piece(part, head), base + 6 * side + 3 + part] = -1.0
    return sel_q, sel_ka, sel_kb


def _fox_prep_kernel(q_ref, k_ref, v_ref, f_ref, bf_ref, gq_ref, gk_ref, grp_ref, tri_ref,
                     sq_ref, ska_ref, skb_ref, qa_ref, ka_ref, kb_ref, vb_ref, dcar_ref):
    @pl.when(pl.program_id(1) == 0)
    def _():
        dcar_ref[...] = jnp.zeros_like(dcar_ref)

    lane = lax.broadcasted_iota(I32, (1, LANES), 1)
    log_f = -_softplus(-(f_ref[...] + bf_ref[...]))
    log_f = jnp.where(lane < FOX_HEADS, log_f, 0.0)
    dcum = jnp.dot(tri_ref[...], log_f, preferred_element_type=F32,
                   precision=lax.Precision.HIGHEST) + dcar_ref[...]
    dcar_ref[...] = dcum[dcum.shape[0] - 1:, :]

    d2 = dcum * LOG2E
    d_hi = d2.astype(BF16)
    rem = d2 - d_hi.astype(F32)
    d_mid = rem.astype(BF16)
    d_lo = (rem - d_mid.astype(F32)).astype(BF16)
    hi, mid, lo = (pltpu.roll(p.astype(F32), s, axis=1) for p, s in
                   ((d_hi, 0), (d_mid, FOX_HEADS), (d_lo, 2 * FOX_HEADS)))
    pieces = jnp.where(lane < FOX_HEADS, hi,
                       jnp.where(lane < 2 * FOX_HEADS, mid,
                                 jnp.where(lane < 3 * FOX_HEADS, lo,
                                           jnp.where(lane == 3 * FOX_HEADS, 1.0, 0.0)))).astype(BF16)
    bias_q = jnp.dot(pieces, sq_ref[...], preferred_element_type=F32).astype(BF16)
    bias_ka = jnp.dot(pieces, ska_ref[...], preferred_element_type=F32).astype(BF16)
    bias_kb = jnp.dot(pieces, skb_ref[...], preferred_element_type=F32).astype(BF16)

    grp = grp_ref[...]
    first_head = lane < FOX_HD

    def rms(xt, gain):
        sq = xt * xt
        sq_hi = sq.astype(BF16)
        sq_lo = (sq - sq_hi.astype(F32)).astype(BF16)
        ssum = (jnp.dot(sq_hi, grp, preferred_element_type=F32)
                + jnp.dot(sq_lo, grp, preferred_element_type=F32))
        return xt * lax.rsqrt(ssum * (1.0 / FOX_HD) + LN_EPS) * gain

    for p in range(FOX_PAIRS):
        src = slice(p * LANES, (p + 1) * LANES)
        dst = slice(p * FOX_AUG, p * FOX_AUG + LANES)
        dst_bias = slice(p * FOX_AUG + LANES, (p + 1) * FOX_AUG)
        qn = rms(q_ref[:, src], gq_ref[...] * (FOX_HD ** -0.5 * LOG2E)).astype(BF16)
        kn = rms(k_ref[:, src], gk_ref[...]).astype(BF16)
        qa_ref[:, dst] = qn
        qa_ref[:, dst_bias] = bias_q[:, src]
        ka_ref[:, dst] = jnp.where(first_head, kn, jnp.zeros_like(kn))
        ka_ref[:, dst_bias] = bias_ka[:, src]
        kb_ref[:, dst] = jnp.where(first_head, jnp.zeros_like(kn), kn)
        kb_ref[:, dst_bias] = bias_kb[:, src]
    vb_ref[...] = v_ref[...].astype(BF16)


def _fox_prep_call(proj, b_f, q_norm_g, k_norm_g, batch, seq):
    t = proj.shape[0]
    d = FOX_HEADS * FOX_HD
    rows = min(FOX_PREP_ROWS, seq)
    per_seq = seq // rows
    row_map = lambda b, j: b * per_seq + j
    sel_q, sel_ka, sel_kb = (jnp.asarray(s, BF16) for s in _fox_bias_selectors())
    grp = np.kron(np.eye(2, dtype=np.float32), np.ones((FOX_HD, FOX_HD), np.float32))
    tri = np.tril(np.ones((rows, rows), np.float32))
    bf_pad = jnp.zeros((1, LANES), F32).at[0, :FOX_HEADS].set(b_f)
    gq2 = jnp.tile(q_norm_g, 2).reshape(1, LANES)
    gk2 = jnp.tile(k_norm_g, 2).reshape(1, LANES)
    const = lambda shape: pl.BlockSpec(shape, lambda b, j: tuple(0 for _ in shape))
    aug = jax.ShapeDtypeStruct((t, FOX_PAIRS * FOX_AUG), BF16)
    aug_spec = pl.BlockSpec((rows, FOX_PAIRS * FOX_AUG), lambda b, j: (row_map(b, j), 0))
    return pl.pallas_call(
        _fox_prep_kernel,
        grid=(batch, per_seq),
        in_specs=[
            pl.BlockSpec((rows, d), lambda b, j: (row_map(b, j), 0)),
            pl.BlockSpec((rows, d), lambda b, j: (row_map(b, j), 1)),
            pl.BlockSpec((rows, d), lambda b, j: (row_map(b, j), 2)),
            pl.BlockSpec((rows, LANES), lambda b, j: (row_map(b, j), 4 * d // LANES)),
            const((1, LANES)), const((1, LANES)), const((1, LANES)),
            const((LANES, LANES)), const((rows, rows)),
            const((LANES, FOX_PAIRS * LANES)), const((LANES, FOX_PAIRS * LANES)),
            const((LANES, FOX_PAIRS * LANES)),
        ],
        out_specs=[aug_spec, aug_spec, aug_spec,
                   pl.BlockSpec((rows, d), lambda b, j: (row_map(b, j), 0))],
        out_shape=[aug, aug, aug, jax.ShapeDtypeStruct((t, d), BF16)],
        scratch_shapes=[pltpu.VMEM((1, LANES), F32)],
        compiler_params=_cparams(("parallel", "arbitrary")),
        name="fox_prep",
    )(proj, proj, proj, proj, bf_pad, gq2, gk2, jnp.asarray(grp, BF16), jnp.asarray(tri),
      sel_q, sel_ka, sel_kb)


def _fox_attn_kernel(qa_ref, ka_ref, kb_ref, v_ref, og_ref, o_ref, m_ref, l_ref, acc_ref):
    qi = pl.program_id(2)
    tq = qa_ref.shape[0]
    sub = lax.broadcasted_iota(I32, (LANES, 1), 0)
    first_head = sub < FOX_HD

    m_ref[...] = jnp.full_like(m_ref, NEG_BIG)
    l_ref[...] = jnp.zeros_like(l_ref)
    acc_ref[...] = jnp.zeros_like(acc_ref)

    def step(ki, masked):
        rows = pl.ds(pl.multiple_of(ki * tq, tq), tq)
        q = qa_ref[...]
        v = v_ref[rows, :]
        pv, alpha = [], []
        sts = [lax.dot_general(k_ref[rows, :], q, (((1,), (1,)), ((), ())), preferred_element_type=F32)
               for k_ref in (ka_ref, kb_ref)]
        for side, st in enumerate(sts):
            if masked:
                key = lax.broadcasted_iota(I32, st.shape, 0)
                qry = lax.broadcasted_iota(I32, st.shape, 1)
                st = jnp.where(key <= qry, st, NEG_BIG)
            m_prev = m_ref[side:side + 1, :]
            m_new = jnp.maximum(m_prev, jnp.max(st, axis=0, keepdims=True))
            a = jnp.exp2(m_prev - m_new)
            p = jnp.exp2(st - m_new)
            l_ref[side:side + 1, :] = a * l_ref[side:side + 1, :] + jnp.sum(p, axis=0, keepdims=True)
            m_ref[side:side + 1, :] = m_new
            pv.append(lax.dot_general(v, p.astype(BF16), (((0,), (0,)), ((), ())), preferred_element_type=F32))
            alpha.append(a)
        acc_ref[...] = (acc_ref[...] * jnp.where(first_head, alpha[0], alpha[1])
                        + jnp.where(first_head, pv[0], pv[1]))

    def body(ki, carry):
        step(ki, False)
        return carry

    lax.fori_loop(0, qi, body, 0)
    step(qi, True)
    out_t = acc_ref[...] / jnp.where(first_head, l_ref[0:1, :], l_ref[1:2, :])
    o_ref[...] = out_t.T * jax.nn.sigmoid(og_ref[...])


def _fox_attn_call(qa, ka, kb, vb, proj, batch, seq):
    t = qa.shape[0]
    d = FOX_HEADS * FOX_HD
    tq = min(FOX_TQ, seq)
    nq = seq // tq
    q_map = lambda b, p, i: (b * nq + i, p)
    kv_map = lambda b, p, i: (b, p)
    return pl.pallas_call(
        _fox_attn_kernel,
        grid=(batch, FOX_PAIRS, nq),
        in_specs=[
            pl.BlockSpec((tq, FOX_AUG), q_map),
            pl.BlockSpec((seq, FOX_AUG), kv_map),
            pl.BlockSpec((seq, FOX_AUG), kv_map),
            pl.BlockSpec((seq, LANES), kv_map),
            pl.BlockSpec((tq, LANES), lambda b, p, i: (b * nq + i, 3 * d // LANES + p)),
        ],
        out_specs=pl.BlockSpec((tq, LANES), q_map),
        out_shape=jax.ShapeDtypeStruct((t, d), F32),
        scratch_shapes=[
            pltpu.VMEM((2, tq), F32),
            pltpu.VMEM((2, tq), F32),
            pltpu.VMEM((LANES, tq), F32),
        ],
        compiler_params=_cparams(("parallel", "parallel", "arbitrary")),
        name="fox_attn",
    )(qa, ka, kb, vb, proj)


def _router_kernel(x_ref, sc_ref, sh_ref, whi_ref, wlo_ref, b_ref, h_ref, idx_ref, wt_ref):
    h = x_ref[...] * (1.0 + sc_ref[0]) + sh_ref[0]
    h_ref[:, 0, :] = h
    h_hi = h.astype(BF16)
    h_lo = (h - h_hi.astype(F32)).astype(BF16)
    logits = (jnp.dot(h_hi, whi_ref[...], preferred_element_type=F32)
              + jnp.dot(h_lo, whi_ref[...], preferred_element_type=F32)
              + jnp.dot(h_hi, wlo_ref[...], preferred_element_type=F32)
              + b_ref[...])
    lane = lax.broadcasted_iota(I32, logits.shape, 1)
    logits = jnp.where(lane < N_EXPERTS, logits, NEG_BIG)
    idx_out = jnp.zeros(logits.shape, I32)
    wt_out = jnp.zeros(logits.shape, F32)
    top0 = None
    denom = None
    for k in range(TOP_K):
        m = jnp.max(logits, axis=-1, keepdims=True)
        sel = jnp.min(jnp.where(logits == m, lane, LANES), axis=-1, keepdims=True)
        logits = jnp.where(lane == sel, NEG_BIG, logits)
        if k == 0:
            top0 = m
        e = jnp.exp(m - top0)
        denom = e if k == 0 else denom + e
        idx_out = jnp.where(lane == k, sel, idx_out)
        wt_out = jnp.where(lane == k, e, wt_out)
    idx_ref[...] = idx_out
    wt_ref[...] = wt_out / denom


def _router_call(x, sc, sh, w_hi, w_lo, b_pad, seq, tm):
    t, d = x.shape
    per_seq = seq // tm
    const = lambda shape: pl.BlockSpec(shape, lambda i: tuple(0 for _ in shape))
    return pl.pallas_call(
        _router_kernel,
        grid=(t // tm,),
        in_specs=[
            pl.BlockSpec((tm, d), lambda i: (i, 0)),
            pl.BlockSpec((1, 1, d), lambda i: (i // per_seq, 0, 0)),
            pl.BlockSpec((1, 1, d), lambda i: (i // per_seq, 0, 0)),
            const((d, LANES)), const((d, LANES)), const((1, LANES)),
        ],
        out_specs=[
            pl.BlockSpec((tm, 1, d), lambda i: (i, 0, 0)),
            pl.BlockSpec((tm, LANES), lambda i: (i, 0)),
            pl.BlockSpec((tm, LANES), lambda i: (i, 0)),
        ],
        out_shape=[
            jax.ShapeDtypeStruct((t, 1, d), F32),
            jax.ShapeDtypeStruct((t, LANES), I32),
            jax.ShapeDtypeStruct((t, LANES), F32),
        ],
        compiler_params=_cparams(("parallel",)),
        name="moe_router",
    )(x, sc, sh, w_hi, w_lo, b_pad)


MOE_GROUP = 2 * LANES


def _moe_kernel(be_ref, src_ref, dst_ref, nact_ref,
                h_hbm, wgu_ref, bgu_ref, wdn_ref, bdn_ref, perm_ref,
                y_hbm,
                xa, xb, xc, ya, yb, yc, wgu_bf, wdn_bf, gsem, ssem):
    i = pl.program_id(0)
    n_act = nact_ref[0]
    rows = MOE_ROWS
    n_groups = wgu_ref.shape[3] // MOE_GROUP
    xbufs, ybufs = (xa, xb, xc), (ya, yb, yc)
    n_buf = len(xbufs)

    def gather_start(blk, r, buf):
        tok = src_ref[blk * rows + r]
        pltpu.make_async_copy(h_hbm.at[tok], xbufs[buf].at[pl.ds(r, 1), :], gsem.at[buf]).start()

    def scatter_start(blk_ext, r, buf):
        row = dst_ref[blk_ext * rows + r]
        pltpu.make_async_copy(ybufs[buf].at[pl.ds(r, 1), :], y_hbm.at[row], ssem.at[buf]).start()

    def wait_gather(buf):
        pltpu.make_async_copy(xbufs[buf], xbufs[buf], gsem.at[buf]).wait()

    def wait_scatter(buf):
        pltpu.make_async_copy(ybufs[buf], ybufs[buf], ssem.at[buf]).wait()

    def rolled(start_fn, blk, buf):
        def body(r, c):
            start_fn(blk, r, buf)
            return c
        lax.fori_loop(0, rows, body, 0)

    @pl.when(i == 0)
    def _():
        rolled(gather_start, 0, 0)
        rolled(gather_start, 1, 1)
        for buf in range(n_buf):
            ybufs[buf][...] = jnp.zeros_like(ybufs[buf])
        for par in range(2):
            def zero_row(r, c, par=par):
                row = dst_ref[r] + (par - 1) * rows
                pltpu.make_async_copy(ybufs[par].at[pl.ds(r, 1), :], y_hbm.at[row], ssem.at[par]).start()
                return c
            lax.fori_loop(0, rows, zero_row, 0)
            wait_scatter(par)

    def block_step(cur):
        prev = (cur + 2) % n_buf
        prev2 = (cur + 1) % n_buf
        wait_gather(cur)
        x = xbufs[cur][...].astype(BF16)
        gu = jnp.dot(x, wgu_bf[...], preferred_element_type=F32) + bgu_ref[0, 0]
        parts = []
        for g in range(n_groups):
            gate = jnp.minimum(gu[:, g * MOE_GROUP:g * MOE_GROUP + LANES], SWIGLU_LIMIT)
            up = jnp.clip(gu[:, g * MOE_GROUP + LANES:(g + 1) * MOE_GROUP], -SWIGLU_LIMIT, SWIGLU_LIMIT)
            glu = gate * jax.nn.sigmoid(SWIGLU_ALPHA * gate)
            parts.append(((up + 1.0) * glu).astype(BF16))
        hmid = jnp.concatenate(parts, axis=-1)
        ybufs[cur][...] = jnp.dot(hmid, wdn_bf[...], preferred_element_type=F32) + bdn_ref[0, 0]
        for r in range(rows):
            gather_start(i + 2, r, prev)
        for r in range(rows):
            scatter_start(i, r, prev)

        @pl.when(i >= 1)
        def _():
            wait_scatter(prev2)

        @pl.when(i == n_act - 1)
        def _():
            rolled(scatter_start, i + 1, cur)
            wait_scatter(prev)
            wait_scatter(cur)
            wait_gather(prev2)
            wait_gather(prev)

    @pl.when(i < n_act)
    def _():
        e_changed = jnp.logical_or(i == 0, be_ref[i] != be_ref[jnp.maximum(i - 1, 0)])

        @pl.when(e_changed)
        def _():
            perm = perm_ref[...]
            for g in range(n_groups):
                cs = slice(g * MOE_GROUP, (g + 1) * MOE_GROUP)
                wgu_bf[:, cs] = jnp.dot(wgu_ref[0, 0, :, cs].astype(BF16), perm,
                                        preferred_element_type=F32).astype(BF16)
            wdn_bf[...] = wdn_ref[0, 0].astype(BF16)

        for buf in range(n_buf):
            @pl.when(i % n_buf == buf)
            def _(buf=buf):
                block_step(buf)


def _moe_call(h, be, src, dst, nact, w_gu, b_gu, w_down, b_down, layer, n_blocks, n_out_rows):
    d = h.shape[-1]
    _, n_exp, _, f2 = w_gu.shape
    f = w_down.shape[2]
    perm = np.zeros((MOE_GROUP, MOE_GROUP), np.float32)
    for jj in range(LANES):
        perm[2 * jj, jj] = 1.0
        perm[2 * jj + 1, LANES + jj] = 1.0
    b_gu_grouped = b_gu[layer].reshape(n_exp, f2 // MOE_GROUP, LANES, 2).transpose(0, 1, 3, 2).reshape(
        1, n_exp, 1, f2)
    by_expert = lambda i, be, s, ds_, na: (layer, be[i], 0, 0)
    grid_spec = pltpu.PrefetchScalarGridSpec(
        num_scalar_prefetch=4,
        grid=(n_blocks,),
        in_specs=[
            pl.BlockSpec(memory_space=pl.ANY),
            pl.BlockSpec((1, 1, d, f2), by_expert),
            pl.BlockSpec((1, 1, 1, f2), lambda i, be, s, ds_, na: (0, be[i], 0, 0)),
            pl.BlockSpec((1, 1, f, d), by_expert),
            pl.BlockSpec((1, 1, 1, d), by_expert),
            pl.BlockSpec((MOE_GROUP, MOE_GROUP), lambda i, be, s, ds_, na: (0, 0)),
        ],
        out_specs=pl.BlockSpec(memory_space=pl.ANY),
        scratch_shapes=[
            *[pltpu.VMEM((MOE_ROWS, d), F32) for _ in range(6)],
            pltpu.VMEM((d, f2), BF16),
            pltpu.VMEM((f, d), BF16),
            pltpu.SemaphoreType.DMA((3,)),
            pltpu.SemaphoreType.DMA((3,)),
        ],
    )
    return pl.pallas_call(
        _moe_kernel,
        grid_spec=grid_spec,
        out_shape=jax.ShapeDtypeStruct((n_out_rows, 1, d), F32),
        compiler_params=_cparams(("arbitrary",)),
        name="moe_experts",
    )(be, src, dst, nact, h, w_gu, b_gu_grouped, w_down, b_down.reshape(b_down.shape[0], n_exp, 1, d),
      jnp.asarray(perm, BF16))


def _combine_kernel(y0_ref, y1_ref, y2_ref, y3_ref, wt_ref, x_ref, g_ref, lg_ref, lb_ref, o_ref):
    wt = wt_ref[...]
    y = y0_ref[:, 0, :] * wt[:, 0:1]
    for k, y_ref in enumerate((y1_ref, y2_ref, y3_ref), start=1):
        y = y + y_ref[:, 0, :] * wt[:, k:k + 1]
    z = ALPHA * x_ref[...] + (1.0 + g_ref[0]) * y
    o_ref[...] = _layer_norm_rows(z, lg_ref[...], lb_ref[...])


def _combine_call(y4, wt, x, gate, ln_g, ln_b, seq, tm):
    t, d = x.shape
    per_seq = seq // tm
    nblk = t // tm
    y_spec = lambda k: pl.BlockSpec((tm, 1, d), lambda i, k=k: (k * nblk + i, 0, 0))
    return pl.pallas_call(
        _combine_kernel,
        grid=(nblk,),
        in_specs=[
            y_spec(0), y_spec(1), y_spec(2), y_spec(3),
            pl.BlockSpec((tm, LANES), lambda i: (i, 0)),
            pl.BlockSpec((tm, d), lambda i: (i, 0)),
            pl.BlockSpec((1, 1, d), lambda i: (i // per_seq, 0, 0)),
            pl.BlockSpec((1, d), lambda i: (0, 0)),
            pl.BlockSpec((1, d), lambda i: (0, 0)),
        ],
        out_specs=pl.BlockSpec((tm, d), lambda i: (i, 0)),
        out_shape=jax.ShapeDtypeStruct((t, d), F32),
        compiler_params=_cparams(("parallel",)),
        name="moe_combine_ln",
    )(y4, y4, y4, y4, wt, x, gate, ln_g.reshape(1, d), ln_b.reshape(1, d))


def _moe_plan(top_idx, t):
    n_assign = t * TOP_K
    n_blocks = -(-(n_assign + N_EXPERTS * (MOE_ROWS - 1)) // MOE_ROWS)
    n_pad = n_blocks * MOE_ROWS
    experts = jnp.arange(N_EXPERTS, dtype=I32)
    flat_e = top_idx.reshape(-1)
    order = jnp.argsort(flat_e).astype(I32)
    counts = jnp.sum((flat_e[:, None] == experts[None, :]).astype(I32), axis=0)
    starts = jnp.cumsum(counts) - counts
    padded = (counts + MOE_ROWS - 1) // MOE_ROWS * MOE_ROWS
    pad_ends = jnp.cumsum(padded)
    pad_starts = pad_ends - padded
    first_row = jnp.arange(n_blocks + 2, dtype=I32) * MOE_ROWS
    block_e = jnp.minimum(jnp.sum((pad_ends[None, :] <= first_row[:, None]).astype(I32), axis=1), N_EXPERTS - 1)
    pos = jnp.arange(n_pad + 2 * MOE_ROWS, dtype=I32)
    e_pos = jnp.repeat(block_e, MOE_ROWS)
    off = pos - pad_starts[e_pos]
    valid = jnp.logical_and(off < counts[e_pos], pos < pad_ends[-1])
    assign = order[jnp.clip(starts[e_pos] + off, 0, n_assign - 1)]
    dump = n_assign + ((pos // MOE_ROWS) % 2) * MOE_ROWS + pos % MOE_ROWS
    src = jnp.where(valid, assign // TOP_K, 0)
    dst = jnp.where(valid, (assign % TOP_K) * t + assign // TOP_K, dump)
    dst_ext = jnp.concatenate([dump[MOE_ROWS:2 * MOE_ROWS], dst[:n_pad]])
    n_act = (pad_ends[-1] // MOE_ROWS).astype(I32).reshape(1)
    return block_e[:n_blocks], src, dst_ext, n_act, n_blocks, n_assign + 2 * MOE_ROWS


def _moe_layer(x, sc, sh, gate, ln_g, ln_b, w_router, b_router, w_gu, b_gu, w_down, b_down, layer, seq, tm):
    t, d = x.shape
    w_pad = jnp.zeros((d, LANES), F32).at[:, :N_EXPERTS].set(w_router)
    w_hi = w_pad.astype(BF16)
    w_lo = (w_pad - w_hi.astype(F32)).astype(BF16)
    b_pad = jnp.zeros((1, LANES), F32).at[0, :N_EXPERTS].set(b_router)
    h, idx, wt = _router_call(x, sc, sh, w_hi, w_lo, b_pad, seq, tm)
    block_e, src, dst, n_act, n_blocks, n_rows = _moe_plan(idx[:, :TOP_K], t)
    y4 = _moe_call(h, block_e, src, dst, n_act, w_gu, b_gu, w_down, b_down, layer, n_blocks, n_rows)
    return _combine_call(y4, wt, x, gate, ln_g, ln_b, seq, tm)


def _pad_cols(w, n):
    return jnp.pad(w, ((0, 0), (0, n - w.shape[1])))


def kernel(x, c, ada_w, ada_b, ln_g, ln_b, gla_w_in, gla_w_gate_up, gla_b_gate, gla_norm_g, gla_w_out, rg_w_in, rg_conv_w, rg_conv_b, rg_w_rg, rg_b_rg, rg_w_ig, rg_b_ig, rg_lambda, rg_w_out, fox_w_in, fox_b_f, fox_q_norm_g, fox_k_norm_g, fox_w_out, moe_w_router, moe_b_router, moe_w_gu, moe_b_gu, moe_w_down, moe_b_down):
    batch, seq, d = x.shape
    t = batch * seq
    tm = min(512, seq)
    depth = ada_w.shape[0]
    xt = x.reshape(t, d)
    mod = _ada_call(c, ada_w, ada_b)

    for l in range(depth):
        part = lambda i: mod[l, :, i * d:(i + 1) * d].reshape(batch, 1, d)
        sh1, sc1, g1, sh2, sc2, g2 = (part(i) for i in range(6))
        kind, j = l % 3, l // 3
        if kind == 0:
            n_in = 2 * GLA_DK + 2 * GLA_DV + LANES
            proj = _modmm_call(xt, sc1, sh1, _pad_cols(gla_w_in[j], n_in).astype(BF16), seq, tm)
            wg = jnp.pad(gla_w_gate_up[j], ((0, LANES - GLA_RANK), (0, 0))).astype(BF16)
            y = _gla_call(proj, wg, gla_b_gate[j], gla_norm_g[j], batch, seq)
            w_out = gla_w_out[j]
        elif kind == 1:
            proj = _modmm_call(xt, sc1, sh1, rg_w_in[j].astype(BF16), seq, tm)
            wgate = jnp.concatenate([rg_w_rg[j], rg_w_ig[j]], axis=-1).astype(BF16)
            y = _rg_call(proj, rg_conv_w[j], rg_conv_b[j], wgate, rg_b_rg[j], rg_b_ig[j],
                         rg_lambda[j], batch, seq)
            w_out = rg_w_out[j]
        else:
            w = fox_w_in[j]
            w = jnp.concatenate([w[:, :3 * d], w[:, 3 * d + FOX_HEADS:], w[:, 3 * d:3 * d + FOX_HEADS]], axis=1)
            proj = _modmm_call(xt, sc1, sh1, _pad_cols(w, 4 * d + LANES).astype(BF16), seq, tm)
            qa, ka, kb, vb = _fox_prep_call(proj, fox_b_f[j], fox_q_norm_g[j], fox_k_norm_g[j], batch, seq)
            y = _fox_attn_call(qa, ka, kb, vb, proj, batch, seq)
            w_out = fox_w_out[j]
        xt = _outln_call(y, w_out.astype(BF16), xt, g1, ln_g[l, 0], ln_b[l, 0], seq, tm)
        xt = _moe_layer(xt, sc2, sh2, g2, ln_g[l, 1], ln_b[l, 1], moe_w_router[l], moe_b_router[l],
                        moe_w_gu, moe_b_gu, moe_w_down, moe_b_down, l, seq, tm)
    return xt.reshape(batch, seq, d)
```

```python
import functools

import numpy as np
import jax
import jax.numpy as jnp
from jax import lax
from jax.experimental import pallas as pl
from jax.experimental.pallas import tpu as pltpu

F32 = jnp.float32
BF16 = jnp.bfloat16
I32 = jnp.int32

DEPTH = 4
ALPHA = (2.0 * DEPTH) ** 0.25
LN_EPS = 1e-5

GLA_HEADS = 4
GLA_HK = 128
GLA_HV = 256
GLA_DK = GLA_HEADS * GLA_HK
GLA_DV = GLA_HEADS * GLA_HV
GLA_RANK = 16
GLA_TAU = 16.0
GLA_CHUNK = 64
GLA_ROWS = 512

RG_BW = 128
RG_BLOCKS = 10
RG_WIDTH = RG_BW * RG_BLOCKS
RG_CONV = 4
RG_C = 8.0
RG_ROWS = 256

FOX_HD = 64
FOX_HEADS = 16
FOX_PAIRS = FOX_HEADS // 2
FOX_TQ = 512
FOX_PREP_ROWS = 256
FOX_AUG = 256

N_EXPERTS = 32
TOP_K = 4
SWIGLU_LIMIT = 7.0
SWIGLU_ALPHA = 1.702
MOE_ROWS = 256
NEG_BIG = -1e30
LOG2E = 1.4426950408889634

LANES = 128
VMEM_LIMIT = 56 * 1024 * 1024


def _cparams(sem, vmem=VMEM_LIMIT):
    return pltpu.CompilerParams(dimension_semantics=sem, vmem_limit_bytes=vmem)


def _softplus(z):
    return jnp.maximum(z, 0.0) + jnp.log(1.0 + jnp.exp(-jnp.abs(z)))


def _layer_norm_rows(z, g, b):
    mu = jnp.mean(z, axis=-1, keepdims=True)
    zc = z - mu
    var = jnp.mean(zc * zc, axis=-1, keepdims=True)
    return zc * lax.rsqrt(var + LN_EPS) * g + b


def _ada_kernel(c_ref, w_ref, b_ref, o_ref):
    c = c_ref[...]
    ca = c * jax.nn.sigmoid(c)
    o_ref[0] = jnp.dot(ca.astype(BF16), w_ref[0].astype(BF16), preferred_element_type=F32) + b_ref[0]


def _ada_call(c, ada_w, ada_b):
    depth, d, n = ada_w.shape
    b = c.shape[0]
    tn = n // 4
    return pl.pallas_call(
        _ada_kernel,
        grid=(depth, n // tn),
        in_specs=[
            pl.BlockSpec((b, d), lambda l, j: (0, 0)),
            pl.BlockSpec((1, d, tn), lambda l, j: (l, 0, j)),
            pl.BlockSpec((1, 1, tn), lambda l, j: (l, 0, j)),
        ],
        out_specs=pl.BlockSpec((1, b, tn), lambda l, j: (l, 0, j)),
        out_shape=jax.ShapeDtypeStruct((depth, b, n), F32),
        compiler_params=_cparams(("parallel", "parallel")),
        name="ada_mod",
    )(c, ada_w, ada_b.reshape(depth, 1, n))


def _modmm_kernel(x_ref, sc_ref, sh_ref, w_ref, o_ref):
    h = x_ref[...] * (1.0 + sc_ref[0]) + sh_ref[0]
    o_ref[...] = jnp.dot(h.astype(BF16), w_ref[...], preferred_element_type=F32)


def _modmm_call(x, sc, sh, w_bf, seq, tm):
    t, d = x.shape
    n = w_bf.shape[1]
    per_seq = seq // tm
    return pl.pallas_call(
        _modmm_kernel,
        grid=(t // tm,),
        in_specs=[
            pl.BlockSpec((tm, d), lambda i: (i, 0)),
            pl.BlockSpec((1, 1, d), lambda i: (i // per_seq, 0, 0)),
            pl.BlockSpec((1, 1, d), lambda i: (i // per_seq, 0, 0)),
            pl.BlockSpec((d, n), lambda i: (0, 0)),
        ],
        out_specs=pl.BlockSpec((tm, n), lambda i: (i, 0)),
        out_shape=jax.ShapeDtypeStruct((t, n), F32),
        compiler_params=_cparams(("parallel",)),
        name="mod_inproj",
    )(x, sc, sh, w_bf)


def _outln_kernel(y_ref, w_ref, x_ref, g_ref, lg_ref, lb_ref, o_ref):
    y = jnp.dot(y_ref[...].astype(BF16), w_ref[...], preferred_element_type=F32)
    z = ALPHA * x_ref[...] + (1.0 + g_ref[0]) * y
    o_ref[...] = _layer_norm_rows(z, lg_ref[...], lb_ref[...])


def _outln_call(y, w_bf, x, gate, ln_g, ln_b, seq, tm):
    t, k = y.shape
    d = x.shape[1]
    per_seq = seq // tm
    return pl.pallas_call(
        _outln_kernel,
        grid=(t // tm,),
        in_specs=[
            pl.BlockSpec((tm, k), lambda i: (i, 0)),
            pl.BlockSpec((k, d), lambda i: (0, 0)),
            pl.BlockSpec((tm, d), lambda i: (i, 0)),
            pl.BlockSpec((1, 1, d), lambda i: (i // per_seq, 0, 0)),
            pl.BlockSpec((1, d), lambda i: (0, 0)),
            pl.BlockSpec((1, d), lambda i: (0, 0)),
        ],
        out_specs=pl.BlockSpec((tm, d), lambda i: (i, 0)),
        out_shape=jax.ShapeDtypeStruct((t, d), F32),
        compiler_params=_cparams(("parallel",)),
        name="outproj_ln",
    )(y, w_bf, x, gate, ln_g.reshape(1, d), ln_b.reshape(1, d))


def _gla_kernel(q_ref, k_ref, v_ref, r_ref, g_ref, wg_ref, bg_ref, ng_ref, tri_ref,
                o_ref, la_ref, st_ref):
    c_rows = GLA_CHUNK
    half = c_rows // 2

    @pl.when(pl.program_id(1) == 0)
    def _():
        st_ref[...] = jnp.zeros_like(st_ref)

    z = jnp.dot(g_ref[...].astype(BF16), wg_ref[...], preferred_element_type=F32) + bg_ref[...]
    la_ref[...] = -_softplus(-z) * (1.0 / GLA_TAU)

    row = lax.broadcasted_iota(I32, (c_rows, c_rows), 0)
    col = lax.broadcasted_iota(I32, (c_rows, c_rows), 1)
    causal = col <= row
    tri = tri_ref[...]
    n_chunks = q_ref.shape[0] // c_rows

    def chunk(c, carry):
        r0 = pl.multiple_of(c * c_rows, c_rows)
        rows = pl.ds(r0, c_rows)
        bc = jnp.dot(tri, la_ref[rows, :], preferred_element_type=F32, precision=lax.Precision.HIGHEST)
        b_mid = bc[half - 1:half, :]
        b_last = bc[c_rows - 1:c_rows, :]
        e_q_mid = jnp.exp(bc - b_mid)
        e_k_mid = jnp.exp(b_mid - bc)
        e_q = jnp.exp(bc)
        e_k_end = jnp.exp(b_last - bc)
        dec = jnp.exp(b_last)
        qc = q_ref[rows, :] * (GLA_HK ** -0.5)
        kc = k_ref[rows, :]
        for h in range(GLA_HEADS):
            ks = slice(h * GLA_HK, (h + 1) * GLA_HK)
            vs = slice(h * GLA_HV, (h + 1) * GLA_HV)
            qm = (qc[:, ks] * e_q_mid[:, ks]).astype(BF16)
            km = (kc[:, ks] * e_k_mid[:, ks]).astype(BF16)
            s = lax.dot_general(qm, km, (((1,), (1,)), ((), ())), preferred_element_type=F32)
            s = jnp.where(causal, s, 0.0)
            vh = v_ref[rows, vs].astype(BF16)
            o = jnp.dot(s.astype(BF16), vh, preferred_element_type=F32)
            qd = (qc[:, ks] * e_q[:, ks]).astype(BF16)
            st = st_ref[h]
            o = o + lax.dot_general(qd, st.astype(BF16), (((1,), (1,)), ((), ())),
                                    preferred_element_type=F32)
            ke = (kc[:, ks] * e_k_end[:, ks]).astype(BF16)
            st_ref[h] = st * dec[:, ks] + lax.dot_general(
                vh, ke, (((0,), (0,)), ((), ())), preferred_element_type=F32)
            ms = jnp.mean(o * o, axis=-1, keepdims=True)
            y = o * lax.rsqrt(ms + LN_EPS) * ng_ref[:, vs]
            rr = r_ref[rows, vs]
            o_ref[rows, vs] = y * (rr * jax.nn.sigmoid(rr))
        return carry

    lax.fori_loop(0, n_chunks, chunk, 0)


def _gla_call(proj, wg_pad_bf, b_gate, norm_g, batch, seq):
    t = proj.shape[0]
    rows = min(GLA_ROWS, seq)
    per_seq = seq // rows
    tri = jnp.asarray(np.tril(np.ones((GLA_CHUNK, GLA_CHUNK), np.float32)))
    row_map = lambda b, j: b * per_seq + j
    return pl.pallas_call(
        _gla_kernel,
        grid=(batch, per_seq),
        in_specs=[
            pl.BlockSpec((rows, GLA_DK), lambda b, j: (row_map(b, j), 0)),
            pl.BlockSpec((rows, GLA_DK), lambda b, j: (row_map(b, j), 1)),
            pl.BlockSpec((rows, GLA_DV), lambda b, j: (row_map(b, j), 1)),
            pl.BlockSpec((rows, GLA_DV), lambda b, j: (row_map(b, j), 2)),
            pl.BlockSpec((rows, LANES), lambda b, j: (row_map(b, j), (2 * GLA_DK + 2 * GLA_DV) // LANES)),
            pl.BlockSpec((LANES, GLA_DK), lambda b, j: (0, 0)),
            pl.BlockSpec((1, GLA_DK), lambda b, j: (0, 0)),
            pl.BlockSpec((1, GLA_DV), lambda b, j: (0, 0)),
            pl.BlockSpec((GLA_CHUNK, GLA_CHUNK), lambda b, j: (0, 0)),
        ],
        out_specs=pl.BlockSpec((rows, GLA_DV), lambda b, j: (row_map(b, j), 0)),
        out_shape=jax.ShapeDtypeStruct((t, GLA_DV), F32),
        scratch_shapes=[
            pltpu.VMEM((rows, GLA_DK), F32),
            pltpu.VMEM((GLA_HEADS, GLA_HV, GLA_HK), F32),
        ],
        compiler_params=_cparams(("parallel", "arbitrary")),
        name="gla_core",
    )(proj, proj, proj, proj, proj, wg_pad_bf, b_gate.reshape(1, GLA_DK), norm_g.reshape(1, GLA_DV), tri)


def _rg_kernel(y_ref, x_ref, cw_ref, cb_ref, wg_ref, brg_ref, big_ref, lam_ref,
               o_ref, xext_ref, hcar_ref):
    rows = x_ref.shape[0]

    @pl.when(pl.program_id(1) == 0)
    def _():
        xext_ref[0:8, :] = jnp.zeros((8, RG_WIDTH), F32)
        hcar_ref[...] = jnp.zeros_like(hcar_ref)

    xext_ref[8:8 + rows, :] = x_ref[...]
    xc = cb_ref[...] + xext_ref[pl.ds(8 - (RG_CONV - 1), rows), :] * cw_ref[0:1, :]
    for j in range(1, RG_CONV):
        xc = xc + xext_ref[pl.ds(8 - (RG_CONV - 1) + j, rows), :] * cw_ref[j:j + 1, :]
    xext_ref[0:8, :] = x_ref[rows - 8:rows, :]

    r_parts, i_parts = [], []
    for n in range(RG_BLOCKS):
        cs = slice(n * RG_BW, (n + 1) * RG_BW)
        g = jnp.dot(xc[:, cs].astype(BF16), wg_ref[n], preferred_element_type=F32)
        r_parts.append(g[:, :RG_BW])
        i_parts.append(g[:, RG_BW:])
    r = jax.nn.sigmoid(jnp.concatenate(r_parts, axis=-1) + brg_ref[...])
    ig = jax.nn.sigmoid(jnp.concatenate(i_parts, axis=-1) + big_ref[...])

    log_a = -RG_C * r * _softplus(-lam_ref[...])
    a = jnp.exp(log_a)
    th = jnp.tanh(log_a)
    u = jnp.sqrt(-2.0 * th / (1.0 - th)) * (ig * xc)

    ridx = lax.broadcasted_iota(I32, (rows, 1), 0)
    acc_a, acc_h = a, u
    d = 1
    while d < rows:
        keep = ridx >= d
        sh_a = pltpu.roll(acc_a, d, axis=0)
        sh_h = pltpu.roll(acc_h, d, axis=0)
        acc_h = jnp.where(keep, acc_a * sh_h + acc_h, acc_h)
        acc_a = jnp.where(keep, acc_a * sh_a, acc_a)
        d *= 2
    h = acc_h + acc_a * hcar_ref[...]
    hcar_ref[...] = h[rows - 1:rows, :]

    yb = y_ref[...]
    gelu = 0.5 * yb * (1.0 + jnp.tanh(0.7978845608028654 * (yb + 0.044715 * (yb * yb * yb))))
    o_ref[...] = h * gelu


def _rg_call(proj, conv_w, conv_b, wgate_bf, b_rg, b_ig, lam, batch, seq):
    t = proj.shape[0]
    rows = min(RG_ROWS, seq)
    per_seq = seq // rows
    row_map = lambda b, j: b * per_seq + j
    vec = lambda: pl.BlockSpec((1, RG_WIDTH), lambda b, j: (0, 0))
    return pl.pallas_call(
        _rg_kernel,
        grid=(batch, per_seq),
        in_specs=[
            pl.BlockSpec((rows, RG_WIDTH), lambda b, j: (row_map(b, j), 0)),
            pl.BlockSpec((rows, RG_WIDTH), lambda b, j: (row_map(b, j), 1)),
            pl.BlockSpec((RG_CONV, RG_WIDTH), lambda b, j: (0, 0)),
            vec(),
            pl.BlockSpec((RG_BLOCKS, RG_BW, 2 * RG_BW), lambda b, j: (0, 0, 0)),
            vec(), vec(), vec(),
        ],
        out_specs=pl.BlockSpec((rows, RG_WIDTH), lambda b, j: (row_map(b, j), 0)),
        out_shape=jax.ShapeDtypeStruct((t, RG_WIDTH), F32),
        scratch_shapes=[
            pltpu.VMEM((rows + 8, RG_WIDTH), F32),
            pltpu.VMEM((1, RG_WIDTH), F32),
        ],
        compiler_params=_cparams(("parallel", "arbitrary")),
        name="rglru_core",
    )(proj, proj, conv_w, conv_b.reshape(1, RG_WIDTH), wgate_bf, b_rg.reshape(1, RG_WIDTH),
      b_ig.reshape(1, RG_WIDTH), lam.reshape(1, RG_WIDTH))


def _fox_bias_selectors():
    piece = lambda part, head: part * FOX_HEADS + head
    one = 3 * FOX_HEADS
    sel_q = np.zeros((LANES, FOX_PAIRS * LANES), np.float32)
    sel_ka = np.zeros((LANES, FOX_PAIRS * LANES), np.float32)
    sel_kb = np.zeros((LANES, FOX_PAIRS * LANES), np.float32)
    for p in range(FOX_PAIRS):
        base = p * LANES
        for side, head in enumerate((2 * p, 2 * p + 1)):
            sel_k = sel_ka if side == 0 else sel_kb
            for part in range(3):
                sel_q[piece(part, head), base + 6 * side + part] = 1.0
                sel_q[one, base + 6 * side + 3 + part] = 1.0
                sel_k[one, base + 6 * side + part] = 1.0
                sel_k[piece(part, head), base + 6 * side + 3 + part] = -1.0
    return sel_q, sel_ka, sel_kb


def _fox_prep_kernel(q_ref, k_ref, v_ref, f_ref, bf_ref, gq_ref, gk_ref, grp_ref, tri_ref,
                     sq_ref, ska_ref, skb_ref, qa_ref, ka_ref, kb_ref, vb_ref, dcar_ref):
    @pl.when(pl.program_id(1) == 0)
    def _():
        dcar_ref[...] = jnp.zeros_like(dcar_ref)

    lane = lax.broadcasted_iota(I32, (1, LANES), 1)
    log_f = -_softplus(-(f_ref[...] + bf_ref[...]))
    log_f = jnp.where(lane < FOX_HEADS, log_f, 0.0)
    dcum = jnp.dot(tri_ref[...], log_f, preferred_element_type=F32,
                   precision=lax.Precision.HIGHEST) + dcar_ref[...]
    dcar_ref[...] = dcum[dcum.shape[0] - 1:, :]

    d2 = dcum * LOG2E
    d_hi = d2.astype(BF16)
    rem = d2 - d_hi.astype(F32)
    d_mid = rem.astype(BF16)
    d_lo = (rem - d_mid.astype(F32)).astype(BF16)
    hi, mid, lo = (pltpu.roll(p.astype(F32), s, axis=1) for p, s in
                   ((d_hi, 0), (d_mid, FOX_HEADS), (d_lo, 2 * FOX_HEADS)))
    pieces = jnp.where(lane < FOX_HEADS, hi,
                       jnp.where(lane < 2 * FOX_HEADS, mid,
                                 jnp.where(lane < 3 * FOX_HEADS, lo,
                                           jnp.where(lane == 3 * FOX_HEADS, 1.0, 0.0)))).astype(BF16)
    bias_q = jnp.dot(pieces, sq_ref[...], preferred_element_type=F32).astype(BF16)
    bias_ka = jnp.dot(pieces, ska_ref[...], preferred_element_type=F32).astype(BF16)
    bias_kb = jnp.dot(pieces, skb_ref[...], preferred_element_type=F32).astype(BF16)

    grp = grp_ref[...]
    first_head = lane < FOX_HD

    def rms(xt, gain):
        sq = xt * xt
        sq_hi = sq.astype(BF16)
        sq_lo = (sq - sq_hi.astype(F32)).astype(BF16)
        ssum = (jnp.dot(sq_hi, grp, preferred_element_type=F32)
                + jnp.dot(sq_lo, grp, preferred_element_type=F32))
        return xt * lax.rsqrt(ssum * (1.0 / FOX_HD) + LN_EPS) * gain

    for p in range(FOX_PAIRS):
        src = slice(p * LANES, (p + 1) * LANES)
        dst = slice(p * FOX_AUG, p * FOX_AUG + LANES)
        dst_bias = slice(p * FOX_AUG + LANES, (p + 1) * FOX_AUG)
        qn = rms(q_ref[:, src], gq_ref[...] * (FOX_HD ** -0.5 * LOG2E)).astype(BF16)
        kn = rms(k_ref[:, src], gk_ref[...]).astype(BF16)
        qa_ref[:, dst] = qn
        qa_ref[:, dst_bias] = bias_q[:, src]
        ka_ref[:, dst] = jnp.where(first_head, kn, jnp.zeros_like(kn))
        ka_ref[:, dst_bias] = bias_ka[:, src]
        kb_ref[:, dst] = jnp.where(first_head, jnp.zeros_like(kn), kn)
        kb_ref[:, dst_bias] = bias_kb[:, src]
    vb_ref[...] = v_ref[...].astype(BF16)


def _fox_prep_call(proj, b_f, q_norm_g, k_norm_g, batch, seq):
    t = proj.shape[0]
    d = FOX_HEADS * FOX_HD
    rows = min(FOX_PREP_ROWS, seq)
    per_seq = seq // rows
    row_map = lambda b, j: b * per_seq + j
    sel_q, sel_ka, sel_kb = (jnp.asarray(s, BF16) for s in _fox_bias_selectors())
    grp = np.kron(np.eye(2, dtype=np.float32), np.ones((FOX_HD, FOX_HD), np.float32))
    tri = np.tril(np.ones((rows, rows), np.float32))
    bf_pad = jnp.zeros((1, LANES), F32).at[0, :FOX_HEADS].set(b_f)
    gq2 = jnp.tile(q_norm_g, 2).reshape(1, LANES)
    gk2 = jnp.tile(k_norm_g, 2).reshape(1, LANES)
    const = lambda shape: pl.BlockSpec(shape, lambda b, j: tuple(0 for _ in shape))
    aug = jax.ShapeDtypeStruct((t, FOX_PAIRS * FOX_AUG), BF16)
    aug_spec = pl.BlockSpec((rows, FOX_PAIRS * FOX_AUG), lambda b, j: (row_map(b, j), 0))
    return pl.pallas_call(
        _fox_prep_kernel,
        grid=(batch, per_seq),
        in_specs=[
            pl.BlockSpec((rows, d), lambda b, j: (row_map(b, j), 0)),
            pl.BlockSpec((rows, d), lambda b, j: (row_map(b, j), 1)),
            pl.BlockSpec((rows, d), lambda b, j: (row_map(b, j), 2)),
            pl.BlockSpec((rows, LANES), lambda b, j: (row_map(b, j), 4 * d // LANES)),
            const((1, LANES)), const((1, LANES)), const((1, LANES)),
            const((LANES, LANES)), const((rows, rows)),
            const((LANES, FOX_PAIRS * LANES)), const((LANES, FOX_PAIRS * LANES)),
            const((LANES, FOX_PAIRS * LANES)),
        ],
        out_specs=[aug_spec, aug_spec, aug_spec,
                   pl.BlockSpec((rows, d), lambda b, j: (row_map(b, j), 0))],
        out_shape=[aug, aug, aug, jax.ShapeDtypeStruct((t, d), BF16)],
        scratch_shapes=[pltpu.VMEM((1, LANES), F32)],
        compiler_params=_cparams(("parallel", "arbitrary")),
        name="fox_prep",
    )(proj, proj, proj, proj, bf_pad, gq2, gk2, jnp.asarray(grp, BF16), jnp.asarray(tri),
      sel_q, sel_ka, sel_kb)


def _fox_attn_kernel(qa_ref, ka_ref, kb_ref, v_ref, og_ref, o_ref, m_ref, l_ref, acc_ref):
    qi = pl.program_id(2)
    tq = qa_ref.shape[0]
    sub = lax.broadcasted_iota(I32, (LANES, 1), 0)
    first_head = sub < FOX_HD

    m_ref[...] = jnp.full_like(m_ref, NEG_BIG)
    l_ref[...] = jnp.zeros_like(l_ref)
    acc_ref[...] = jnp.zeros_like(acc_ref)

    def steps(blocks):
        q = qa_ref[...]
        scored = []
        for ki, masked in blocks:
            rows = pl.ds(pl.multiple_of(ki * tq, tq), tq)
            scored.append((rows, masked, [
                lax.dot_general(k_ref[rows, :], q, (((1,), (1,)), ((), ())), preferred_element_type=F32)
                for k_ref in (ka_ref, kb_ref)]))
        for rows, masked, sts in scored:
            v = v_ref[rows, :]
            pv, alpha = [], []
            for side, st in enumerate(sts):
                if masked:
                    key = lax.broadcasted_iota(I32, st.shape, 0)
                    qry = lax.broadcasted_iota(I32, st.shape, 1)
                    st = jnp.where(key <= qry, st, NEG_BIG)
                m_prev = m_ref[side:side + 1, :]
                m_new = jnp.maximum(m_prev, jnp.max(st, axis=0, keepdims=True))
                a = jnp.exp2(m_prev - m_new)
                p = jnp.exp2(st - m_new)
                l_ref[side:side + 1, :] = a * l_ref[side:side + 1, :] + jnp.sum(p, axis=0, keepdims=True)
                m_ref[side:side + 1, :] = m_new
                pv.append(lax.dot_general(v, p.astype(BF16), (((0,), (0,)), ((), ())),
                                          preferred_element_type=F32))
                alpha.append(a)
            acc_ref[...] = (acc_ref[...] * jnp.where(first_head, alpha[0], alpha[1])
                            + jnp.where(first_head, pv[0], pv[1]))

    def body(j, carry):
        steps([(2 * j, False), (2 * j + 1, False)])
        return carry

    lax.fori_loop(0, qi // 2, body, 0)

    @pl.when(qi % 2 == 1)
    def _():
        steps([(qi - 1, False), (qi, True)])

    @pl.when(qi % 2 == 0)
    def _():
        steps([(qi, True)])

    out_t = acc_ref[...] / jnp.where(first_head, l_ref[0:1, :], l_ref[1:2, :])
    o_ref[...] = out_t.T * jax.nn.sigmoid(og_ref[...])


def _fox_attn_call(qa, ka, kb, vb, proj, batch, seq):
    t = qa.shape[0]
    d = FOX_HEADS * FOX_HD
    tq = min(FOX_TQ, seq)
    nq = seq // tq
    q_map = lambda b, p, i: (b * nq + i, p)
    kv_map = lambda b, p, i: (b, p)
    return pl.pallas_call(
        _fox_attn_kernel,
        grid=(batch, FOX_PAIRS, nq),
        in_specs=[
            pl.BlockSpec((tq, FOX_AUG), q_map),
            pl.BlockSpec((seq, FOX_AUG), kv_map),
            pl.BlockSpec((seq, FOX_AUG), kv_map),
            pl.BlockSpec((seq, LANES), kv_map),
            pl.BlockSpec((tq, LANES), lambda b, p, i: (b * nq + i, 3 * d // LANES + p)),
        ],
        out_specs=pl.BlockSpec((tq, LANES), q_map),
        out_shape=jax.ShapeDtypeStruct((t, d), F32),
        scratch_shapes=[
            pltpu.VMEM((2, tq), F32),
            pltpu.VMEM((2, tq), F32),
            pltpu.VMEM((LANES, tq), F32),
        ],
        compiler_params=_cparams(("parallel", "parallel", "arbitrary")),
        name="fox_attn",
    )(qa, ka, kb, vb, proj)


def _router_kernel(x_ref, sc_ref, sh_ref, whi_ref, wlo_ref, b_ref, h_ref, idx_ref, wt_ref, cnt_ref):
    h = x_ref[...] * (1.0 + sc_ref[0]) + sh_ref[0]
    h_ref[:, 0, :] = h
    h_hi = h.astype(BF16)
    h_lo = (h - h_hi.astype(F32)).astype(BF16)
    logits = (jnp.dot(h_hi, whi_ref[...], preferred_element_type=F32)
              + jnp.dot(h_lo, whi_ref[...], preferred_element_type=F32)
              + jnp.dot(h_hi, wlo_ref[...], preferred_element_type=F32)
              + b_ref[...])
    lane = lax.broadcasted_iota(I32, logits.shape, 1)
    logits = jnp.where(lane < N_EXPERTS, logits, NEG_BIG)
    idx_out = jnp.zeros(logits.shape, I32)
    wt_out = jnp.zeros(logits.shape, F32)
    picked = jnp.zeros(logits.shape, I32)
    top0 = None
    denom = None
    for k in range(TOP_K):
        m = jnp.max(logits, axis=-1, keepdims=True)
        sel = jnp.min(jnp.where(logits == m, lane, LANES), axis=-1, keepdims=True)
        hit = lane == sel
        picked = picked + hit.astype(I32)
        logits = jnp.where(hit, NEG_BIG, logits)
        if k == 0:
            top0 = m
        e = jnp.exp(m - top0)
        denom = e if k == 0 else denom + e
        idx_out = jnp.where(lane == k, sel, idx_out)
        wt_out = jnp.where(lane == k, e, wt_out)
    idx_ref[...] = idx_out
    wt_ref[...] = wt_out / denom
    cnt_ref[0] = jnp.sum(picked, axis=0, keepdims=True)


def _router_call(x, sc, sh, w_hi, w_lo, b_pad, seq, tm):
    t, d = x.shape
    per_seq = seq // tm
    const = lambda shape: pl.BlockSpec(shape, lambda i: tuple(0 for _ in shape))
    return pl.pallas_call(
        _router_kernel,
        grid=(t // tm,),
        in_specs=[
            pl.BlockSpec((tm, d), lambda i: (i, 0)),
            pl.BlockSpec((1, 1, d), lambda i: (i // per_seq, 0, 0)),
            pl.BlockSpec((1, 1, d), lambda i: (i // per_seq, 0, 0)),
            const((d, LANES)), const((d, LANES)), const((1, LANES)),
        ],
        out_specs=[
            pl.BlockSpec((tm, 1, d), lambda i: (i, 0, 0)),
            pl.BlockSpec((tm, LANES), lambda i: (i, 0)),
            pl.BlockSpec((tm, LANES), lambda i: (i, 0)),
            pl.BlockSpec((1, 1, LANES), lambda i: (i, 0, 0)),
        ],
        out_shape=[
            jax.ShapeDtypeStruct((t, 1, d), F32),
            jax.ShapeDtypeStruct((t, LANES), I32),
            jax.ShapeDtypeStruct((t, LANES), F32),
            jax.ShapeDtypeStruct((t // tm, 1, LANES), I32),
        ],
        compiler_params=_cparams(("parallel",)),
        name="moe_router",
    )(x, sc, sh, w_hi, w_lo, b_pad)


MOE_GROUP = 2 * LANES


def _moe_kernel(be_ref, src_ref, dst_ref, nact_ref,
                h_hbm, wgu_ref, bgu_ref, wdn_ref, bdn_ref, perm_ref,
                y_hbm,
                xa, xb, xc, ya, yb, yc, wgu_bf, wdn_bf, gsem, ssem):
    i = pl.program_id(0)
    n_act = nact_ref[0]
    rows = MOE_ROWS
    n_groups = wgu_ref.shape[3] // MOE_GROUP
    xbufs, ybufs = (xa, xb, xc), (ya, yb, yc)
    n_buf = len(xbufs)

    def gather_start(blk, r, buf):
        tok = src_ref[blk * rows + r]
        pltpu.make_async_copy(h_hbm.at[tok], xbufs[buf].at[pl.ds(r, 1), :], gsem.at[buf]).start()

    def scatter_start(blk_ext, r, buf):
        row = dst_ref[blk_ext * rows + r]
        pltpu.make_async_copy(ybufs[buf].at[pl.ds(r, 1), :], y_hbm.at[row], ssem.at[buf]).start(priority=1)

    def wait_gather(buf):
        pltpu.make_async_copy(xbufs[buf], xbufs[buf], gsem.at[buf]).wait()

    def wait_scatter(buf):
        pltpu.make_async_copy(ybufs[buf], ybufs[buf], ssem.at[buf]).wait()

    def rolled(start_fn, blk, buf):
        def body(r, c):
            start_fn(blk, r, buf)
            return c
        lax.fori_loop(0, rows, body, 0)

    @pl.when(i == 0)
    def _():
        rolled(gather_start, 0, 0)
        rolled(gather_start, 1, 1)
        for buf in range(n_buf):
            ybufs[buf][...] = jnp.zeros_like(ybufs[buf])
        for par in range(2):
            def zero_row(r, c, par=par):
                row = dst_ref[r] + (par - 1) * rows
                pltpu.make_async_copy(ybufs[par].at[pl.ds(r, 1), :], y_hbm.at[row], ssem.at[par]).start()
                return c
            lax.fori_loop(0, rows, zero_row, 0)
            wait_scatter(par)

    def block_step(cur):
        prev = (cur + 2) % n_buf
        prev2 = (cur + 1) % n_buf
        wait_gather(cur)
        x = xbufs[cur][...].astype(BF16)
        gu = jnp.dot(x, wgu_bf[...], preferred_element_type=F32) + bgu_ref[0, 0]
        parts = []
        for g in range(n_groups):
            gate = jnp.minimum(gu[:, g * MOE_GROUP:g * MOE_GROUP + LANES], SWIGLU_LIMIT)
            up = jnp.clip(gu[:, g * MOE_GROUP + LANES:(g + 1) * MOE_GROUP], -SWIGLU_LIMIT, SWIGLU_LIMIT)
            glu = gate * jax.nn.sigmoid(SWIGLU_ALPHA * gate)
            parts.append(((up + 1.0) * glu).astype(BF16))
        hmid = jnp.concatenate(parts, axis=-1)
        ybufs[cur][...] = jnp.dot(hmid, wdn_bf[...], preferred_element_type=F32) + bdn_ref[0, 0]
        for r in range(rows):
            gather_start(i + 2, r, prev)
        for r in range(rows):
            scatter_start(i, r, prev)

        @pl.when(i >= 1)
        def _():
            wait_scatter(prev2)

        @pl.when(i == n_act - 1)
        def _():
            rolled(scatter_start, i + 1, cur)
            wait_scatter(prev)
            wait_scatter(cur)
            wait_gather(prev2)
            wait_gather(prev)

    @pl.when(i < n_act)
    def _():
        e_changed = jnp.logical_or(i == 0, be_ref[i] != be_ref[jnp.maximum(i - 1, 0)])

        @pl.when(e_changed)
        def _():
            perm = perm_ref[...]
            for g in range(n_groups):
                cs = slice(g * MOE_GROUP, (g + 1) * MOE_GROUP)
                wgu_bf[:, cs] = jnp.dot(wgu_ref[0, 0, :, cs].astype(BF16), perm,
                                        preferred_element_type=F32).astype(BF16)
            wdn_bf[...] = wdn_ref[0, 0].astype(BF16)

        for buf in range(n_buf):
            @pl.when(i % n_buf == buf)
            def _(buf=buf):
                block_step(buf)


def _moe_call(h, be, src, dst, nact, w_gu, b_gu, w_down, b_down, layer, n_blocks, n_out_rows):
    d = h.shape[-1]
    _, n_exp, _, f2 = w_gu.shape
    f = w_down.shape[2]
    perm = np.zeros((MOE_GROUP, MOE_GROUP), np.float32)
    for jj in range(LANES):
        perm[2 * jj, jj] = 1.0
        perm[2 * jj + 1, LANES + jj] = 1.0
    b_gu_grouped = b_gu[layer].reshape(n_exp, f2 // MOE_GROUP, LANES, 2).transpose(0, 1, 3, 2).reshape(
        1, n_exp, 1, f2)
    by_expert = lambda i, be, s, ds_, na: (layer, be[i], 0, 0)
    grid_spec = pltpu.PrefetchScalarGridSpec(
        num_scalar_prefetch=4,
        grid=(n_blocks,),
        in_specs=[
            pl.BlockSpec(memory_space=pl.ANY),
            pl.BlockSpec((1, 1, d, f2), by_expert),
            pl.BlockSpec((1, 1, 1, f2), lambda i, be, s, ds_, na: (0, be[i], 0, 0)),
            pl.BlockSpec((1, 1, f, d), by_expert),
            pl.BlockSpec((1, 1, 1, d), by_expert),
            pl.BlockSpec((MOE_GROUP, MOE_GROUP), lambda i, be, s, ds_, na: (0, 0)),
        ],
        out_specs=pl.BlockSpec(memory_space=pl.ANY),
        scratch_shapes=[
            *[pltpu.VMEM((MOE_ROWS, d), F32) for _ in range(6)],
            pltpu.VMEM((d, f2), BF16),
            pltpu.VMEM((f, d), BF16),
            pltpu.SemaphoreType.DMA((3,)),
            pltpu.SemaphoreType.DMA((3,)),
        ],
    )
    return pl.pallas_call(
        _moe_kernel,
        grid_spec=grid_spec,
        out_shape=jax.ShapeDtypeStruct((n_out_rows, 1, d), F32),
        compiler_params=_cparams(("arbitrary",)),
        name="moe_experts",
    )(be, src, dst, nact, h, w_gu, b_gu_grouped, w_down, b_down.reshape(b_down.shape[0], n_exp, 1, d),
      jnp.asarray(perm, BF16))


def _combine_kernel(y0_ref, y1_ref, y2_ref, y3_ref, wt_ref, x_ref, g_ref, lg_ref, lb_ref, o_ref):
    wt = wt_ref[...]
    y = y0_ref[:, 0, :] * wt[:, 0:1]
    for k, y_ref in enumerate((y1_ref, y2_ref, y3_ref), start=1):
        y = y + y_ref[:, 0, :] * wt[:, k:k + 1]
    z = ALPHA * x_ref[...] + (1.0 + g_ref[0]) * y
    o_ref[...] = _layer_norm_rows(z, lg_ref[...], lb_ref[...])


def _combine_call(y4, wt, x, gate, ln_g, ln_b, seq, tm):
    t, d = x.shape
    per_seq = seq // tm
    nblk = t // tm
    y_spec = lambda k: pl.BlockSpec((tm, 1, d), lambda i, k=k: (k * nblk + i, 0, 0))
    return pl.pallas_call(
        _combine_kernel,
        grid=(nblk,),
        in_specs=[
            y_spec(0), y_spec(1), y_spec(2), y_spec(3),
            pl.BlockSpec((tm, LANES), lambda i: (i, 0)),
            pl.BlockSpec((tm, d), lambda i: (i, 0)),
            pl.BlockSpec((1, 1, d), lambda i: (i // per_seq, 0, 0)),
            pl.BlockSpec((1, d), lambda i: (0, 0)),
            pl.BlockSpec((1, d), lambda i: (0, 0)),
        ],
        out_specs=pl.BlockSpec((tm, d), lambda i: (i, 0)),
        out_shape=jax.ShapeDtypeStruct((t, d), F32),
        compiler_params=_cparams(("parallel",)),
        name="moe_combine_ln",
    )(y4, y4, y4, y4, wt, x, gate, ln_g.reshape(1, d), ln_b.reshape(1, d))


def _moe_plan(top_idx, block_counts, t):
    n_assign = t * TOP_K
    n_blocks = -(-(n_assign + N_EXPERTS * (MOE_ROWS - 1)) // MOE_ROWS)
    experts = jnp.arange(N_EXPERTS, dtype=I32)
    order = jnp.argsort(top_idx.reshape(-1)).astype(I32)
    counts = jnp.sum(block_counts[:, 0, :N_EXPERTS], axis=0)
    starts = jnp.cumsum(counts) - counts
    padded = (counts + MOE_ROWS - 1) // MOE_ROWS * MOE_ROWS
    pad_ends = jnp.cumsum(padded)
    pad_starts = pad_ends - padded
    blocks = jnp.arange(n_blocks + 2, dtype=I32)
    first_row = blocks * MOE_ROWS
    block_e = jnp.minimum(jnp.sum((pad_ends[None, :] <= first_row[:, None]).astype(I32), axis=1), N_EXPERTS - 1)
    of_block = lambda v: jnp.sum(jnp.where(block_e[:, None] == experts[None, :], v[None, :], 0), axis=1)
    seg_off = first_row - of_block(pad_starts)
    n_valid = jnp.where(first_row < pad_ends[-1], jnp.clip(of_block(counts) - seg_off, 0, MOE_ROWS), 0)
    r = jnp.arange(MOE_ROWS, dtype=I32)
    valid = r[None, :] < n_valid[:, None]
    assign = order[jnp.clip((of_block(starts) + seg_off)[:, None] + r[None, :], 0, n_assign - 1)]
    dump = n_assign + (blocks[:, None] % 2) * MOE_ROWS + r[None, :]
    src = jnp.where(valid, assign // TOP_K, 0).reshape(-1)
    dst = jnp.where(valid, (assign % TOP_K) * t + assign // TOP_K, dump)
    dst_ext = jnp.concatenate([dump[1], dst[:n_blocks].reshape(-1)])
    n_act = (pad_ends[-1] // MOE_ROWS).astype(I32).reshape(1)
    return block_e[:n_blocks], src, dst_ext, n_act, n_blocks, n_assign + 2 * MOE_ROWS


def _moe_layer(x, sc, sh, gate, ln_g, ln_b, w_router, b_router, w_gu, b_gu, w_down, b_down, layer, seq, tm):
    t, d = x.shape
    w_pad = jnp.zeros((d, LANES), F32).at[:, :N_EXPERTS].set(w_router)
    w_hi = w_pad.astype(BF16)
    w_lo = (w_pad - w_hi.astype(F32)).astype(BF16)
    b_pad = jnp.zeros((1, LANES), F32).at[0, :N_EXPERTS].set(b_router)
    h, idx, wt, block_counts = _router_call(x, sc, sh, w_hi, w_lo, b_pad, seq, tm)
    block_e, src, dst, n_act, n_blocks, n_rows = _moe_plan(idx[:, :TOP_K], block_counts, t)
    y4 = _moe_call(h, block_e, src, dst, n_act, w_gu, b_gu, w_down, b_down, layer, n_blocks, n_rows)
    return _combine_call(y4, wt, x, gate, ln_g, ln_b, seq, tm)


def _pad_cols(w, n):
    return jnp.pad(w, ((0, 0), (0, n - w.shape[1])))


def kernel(x, c, ada_w, ada_b, ln_g, ln_b, gla_w_in, gla_w_gate_up, gla_b_gate, gla_norm_g, gla_w_out, rg_w_in, rg_conv_w, rg_conv_b, rg_w_rg, rg_b_rg, rg_w_ig, rg_b_ig, rg_lambda, rg_w_out, fox_w_in, fox_b_f, fox_q_norm_g, fox_k_norm_g, fox_w_out, moe_w_router, moe_b_router, moe_w_gu, moe_b_gu, moe_w_down, moe_b_down):
    batch, seq, d = x.shape
    t = batch * seq
    tm = min(512, seq)
    depth = ada_w.shape[0]
    xt = x.reshape(t, d)
    mod = _ada_call(c, ada_w, ada_b)

    for l in range(depth):
        part = lambda i: mod[l, :, i * d:(i + 1) * d].reshape(batch, 1, d)
        sh1, sc1, g1, sh2, sc2, g2 = (part(i) for i in range(6))
        kind, j = l % 3, l // 3
        if kind == 0:
            n_in = 2 * GLA_DK + 2 * GLA_DV + LANES
            proj = _modmm_call(xt, sc1, sh1, _pad_cols(gla_w_in[j], n_in).astype(BF16), seq, tm)
            wg = jnp.pad(gla_w_gate_up[j], ((0, LANES - GLA_RANK), (0, 0))).astype(BF16)
            y = _gla_call(proj, wg, gla_b_gate[j], gla_norm_g[j], batch, seq)
            w_out = gla_w_out[j]
        elif kind == 1:
            proj = _modmm_call(xt, sc1, sh1, rg_w_in[j].astype(BF16), seq, tm)
            wgate = jnp.concatenate([rg_w_rg[j], rg_w_ig[j]], axis=-1).astype(BF16)
            y = _rg_call(proj, rg_conv_w[j], rg_conv_b[j], wgate, rg_b_rg[j], rg_b_ig[j],
                         rg_lambda[j], batch, seq)
            w_out = rg_w_out[j]
        else:
            w = fox_w_in[j]
            w = jnp.concatenate([w[:, :3 * d], w[:, 3 * d + FOX_HEADS:], w[:, 3 * d:3 * d + FOX_HEADS]], axis=1)
            proj = _modmm_call(xt, sc1, sh1, _pad_cols(w, 4 * d + LANES).astype(BF16), seq, tm)
            qa, ka, kb, vb = _fox_prep_call(proj, fox_b_f[j], fox_q_norm_g[j], fox_k_norm_g[j], batch, seq)
            y = _fox_attn_call(qa, ka, kb, vb, proj, batch, seq)
            w_out = fox_w_out[j]
        xt = _outln_call(y, w_out.astype(BF16), xt, g1, ln_g[l, 0], ln_b[l, 0], seq, tm)
        xt = _moe_layer(xt, sc2, sh2, g2, ln_g[l, 1], ln_b[l, 1], moe_w_router[l], moe_b_router[l],
                        moe_w_gu, moe_b_gu, moe_w_down, moe_b_down, l, seq, tm)
    return xt.reshape(batch, seq, d)
```

```python
import functools

import numpy as np
import jax
import jax.numpy as jnp
from jax import lax
from jax.experimental import pallas as pl
from jax.experimental.pallas import tpu as pltpu

F32 = jnp.float32
BF16 = jnp.bfloat16
I32 = jnp.int32

DEPTH = 4
ALPHA = (2.0 * DEPTH) ** 0.25
LN_EPS = 1e-5

GLA_HEADS = 4
GLA_HK = 128
GLA_HV = 256
GLA_DK = GLA_HEADS * GLA_HK
GLA_DV = GLA_HEADS * GLA_HV
GLA_RANK = 16
GLA_TAU = 16.0
GLA_CHUNK = 64
GLA_ROWS = 512

RG_BW = 128
RG_BLOCKS = 10
RG_WIDTH = RG_BW * RG_BLOCKS
RG_CONV = 4
RG_C = 8.0
RG_ROWS = 256

FOX_HD = 64
FOX_HEADS = 16
FOX_PAIRS = FOX_HEADS // 2
FOX_TQ = 512
FOX_PREP_ROWS = 256
FOX_AUG = 256

N_EXPERTS = 32
TOP_K = 4
SWIGLU_LIMIT = 7.0
SWIGLU_ALPHA = 1.702
MOE_ROWS = 256
NEG_BIG = -1e30
LOG2E = 1.4426950408889634

LANES = 128
VMEM_LIMIT = 56 * 1024 * 1024


def _cparams(sem, vmem=VMEM_LIMIT):
    return pltpu.CompilerParams(dimension_semantics=sem, vmem_limit_bytes=vmem)


def _softplus(z):
    return jnp.maximum(z, 0.0) + jnp.log(1.0 + jnp.exp(-jnp.abs(z)))


def _layer_norm_rows(z, g, b):
    mu = jnp.mean(z, axis=-1, keepdims=True)
    zc = z - mu
    var = jnp.mean(zc * zc, axis=-1, keepdims=True)
    return zc * lax.rsqrt(var + LN_EPS) * g + b


def _ada_kernel(c_ref, w_ref, b_ref, o_ref):
    c = c_ref[...]
    ca = c * jax.nn.sigmoid(c)
    o_ref[0] = jnp.dot(ca.astype(BF16), w_ref[0].astype(BF16), preferred_element_type=F32) + b_ref[0]


def _ada_call(c, ada_w, ada_b):
    depth, d, n = ada_w.shape
    b = c.shape[0]
    tn = n // 4
    return pl.pallas_call(
        _ada_kernel,
        grid=(depth, n // tn),
        in_specs=[
            pl.BlockSpec((b, d), lambda l, j: (0, 0)),
            pl.BlockSpec((1, d, tn), lambda l, j: (l, 0, j)),
            pl.BlockSpec((1, 1, tn), lambda l, j: (l, 0, j)),
        ],
        out_specs=pl.BlockSpec((1, b, tn), lambda l, j: (l, 0, j)),
        out_shape=jax.ShapeDtypeStruct((depth, b, n), F32),
        compiler_params=_cparams(("parallel", "parallel")),
        name="ada_mod",
    )(c, ada_w, ada_b.reshape(depth, 1, n))


def _modmm_kernel(x_ref, sc_ref, sh_ref, w_ref, o_ref):
    h = x_ref[...] * (1.0 + sc_ref[0]) + sh_ref[0]
    o_ref[...] = jnp.dot(h.astype(BF16), w_ref[...], preferred_element_type=F32)


def _modmm_call(x, sc, sh, w_bf, seq, tm):
    t, d = x.shape
    n = w_bf.shape[1]
    per_seq = seq // tm
    return pl.pallas_call(
        _modmm_kernel,
        grid=(t // tm,),
        in_specs=[
            pl.BlockSpec((tm, d), lambda i: (i, 0)),
            pl.BlockSpec((1, 1, d), lambda i: (i // per_seq, 0, 0)),
            pl.BlockSpec((1, 1, d), lambda i: (i // per_seq, 0, 0)),
            pl.BlockSpec((d, n), lambda i: (0, 0)),
        ],
        out_specs=pl.BlockSpec((tm, n), lambda i: (i, 0)),
        out_shape=jax.ShapeDtypeStruct((t, n), F32),
        compiler_params=_cparams(("parallel",)),
        name="mod_inproj",
    )(x, sc, sh, w_bf)


def _outln_kernel(y_ref, w_ref, x_ref, g_ref, lg_ref, lb_ref, o_ref):
    y = jnp.dot(y_ref[...].astype(BF16), w_ref[...], preferred_element_type=F32)
    z = ALPHA * x_ref[...] + (1.0 + g_ref[0]) * y
    o_ref[...] = _layer_norm_rows(z, lg_ref[...], lb_ref[...])


def _outln_call(y, w_bf, x, gate, ln_g, ln_b, seq, tm):
    t, k = y.shape
    d = x.shape[1]
    per_seq = seq // tm
    return pl.pallas_call(
        _outln_kernel,
        grid=(t // tm,),
        in_specs=[
            pl.BlockSpec((tm, k), lambda i: (i, 0)),
            pl.BlockSpec((k, d), lambda i: (0, 0)),
            pl.BlockSpec((tm, d), lambda i: (i, 0)),
            pl.BlockSpec((1, 1, d), lambda i: (i // per_seq, 0, 0)),
            pl.BlockSpec((1, d), lambda i: (0, 0)),
            pl.BlockSpec((1, d), lambda i: (0, 0)),
        ],
        out_specs=pl.BlockSpec((tm, d), lambda i: (i, 0)),
        out_shape=jax.ShapeDtypeStruct((t, d), F32),
        compiler_params=_cparams(("parallel",)),
        name="outproj_ln",
    )(y, w_bf, x, gate, ln_g.reshape(1, d), ln_b.reshape(1, d))


def _gla_kernel(q_ref, k_ref, v_ref, r_ref, g_ref, wg_ref, bg_ref, ng_ref, tri_ref,
                o_ref, la_ref, st_ref):
    c_rows = GLA_CHUNK
    half = c_rows // 2

    @pl.when(pl.program_id(1) == 0)
    def _():
        st_ref[...] = jnp.zeros_like(st_ref)

    z = jnp.dot(g_ref[...].astype(BF16), wg_ref[...], preferred_element_type=F32) + bg_ref[...]
    la_ref[...] = -_softplus(-z) * (1.0 / GLA_TAU)

    row = lax.broadcasted_iota(I32, (c_rows, c_rows), 0)
    col = lax.broadcasted_iota(I32, (c_rows, c_rows), 1)
    causal = col <= row
    tri = tri_ref[...]
    n_chunks = q_ref.shape[0] // c_rows

    def chunk(c):
        rows = pl.ds(c * c_rows, c_rows)
        bc = jnp.dot(tri, la_ref[rows, :], preferred_element_type=F32, precision=lax.Precision.HIGHEST)
        b_mid = bc[half - 1:half, :]
        b_last = bc[c_rows - 1:c_rows, :]
        e_q_mid = jnp.exp(bc - b_mid)
        e_k_mid = jnp.exp(b_mid - bc)
        e_q = jnp.exp(bc)
        e_k_end = jnp.exp(b_last - bc)
        dec = jnp.exp(b_last)
        qc = q_ref[rows, :] * (GLA_HK ** -0.5)
        kc = k_ref[rows, :]
        for h in range(GLA_HEADS):
            ks = slice(h * GLA_HK, (h + 1) * GLA_HK)
            vs = slice(h * GLA_HV, (h + 1) * GLA_HV)
            qm = (qc[:, ks] * e_q_mid[:, ks]).astype(BF16)
            km = (kc[:, ks] * e_k_mid[:, ks]).astype(BF16)
            s = lax.dot_general(qm, km, (((1,), (1,)), ((), ())), preferred_element_type=F32)
            s = jnp.where(causal, s, 0.0)
            vh = v_ref[rows, vs].astype(BF16)
            o = jnp.dot(s.astype(BF16), vh, preferred_element_type=F32)
            qd = (qc[:, ks] * e_q[:, ks]).astype(BF16)
            st = st_ref[h]
            o = o + lax.dot_general(qd, st.astype(BF16), (((1,), (1,)), ((), ())),
                                    preferred_element_type=F32)
            ke = (kc[:, ks] * e_k_end[:, ks]).astype(BF16)
            st_ref[h] = st * dec[:, ks] + lax.dot_general(
                vh, ke, (((0,), (0,)), ((), ())), preferred_element_type=F32)
            ms = jnp.mean(o * o, axis=-1, keepdims=True)
            y = o * lax.rsqrt(ms + LN_EPS) * ng_ref[:, vs]
            rr = r_ref[rows, vs]
            o_ref[rows, vs] = y * (rr * jax.nn.sigmoid(rr))

    for c in range(n_chunks):
        chunk(c)


def _gla_call(proj, wg_pad_bf, b_gate, norm_g, batch, seq):
    t = proj.shape[0]
    rows = min(GLA_ROWS, seq)
    per_seq = seq // rows
    tri = jnp.asarray(np.tril(np.ones((GLA_CHUNK, GLA_CHUNK), np.float32)))
    row_map = lambda b, j: b * per_seq + j
    return pl.pallas_call(
        _gla_kernel,
        grid=(batch, per_seq),
        in_specs=[
            pl.BlockSpec((rows, GLA_DK), lambda b, j: (row_map(b, j), 0)),
            pl.BlockSpec((rows, GLA_DK), lambda b, j: (row_map(b, j), 1)),
            pl.BlockSpec((rows, GLA_DV), lambda b, j: (row_map(b, j), 1)),
            pl.BlockSpec((rows, GLA_DV), lambda b, j: (row_map(b, j), 2)),
            pl.BlockSpec((rows, LANES), lambda b, j: (row_map(b, j), (2 * GLA_DK + 2 * GLA_DV) // LANES)),
            pl.BlockSpec((LANES, GLA_DK), lambda b, j: (0, 0)),
            pl.BlockSpec((1, GLA_DK), lambda b, j: (0, 0)),
            pl.BlockSpec((1, GLA_DV), lambda b, j: (0, 0)),
            pl.BlockSpec((GLA_CHUNK, GLA_CHUNK), lambda b, j: (0, 0)),
        ],
        out_specs=pl.BlockSpec((rows, GLA_DV), lambda b, j: (row_map(b, j), 0)),
        out_shape=jax.ShapeDtypeStruct((t, GLA_DV), F32),
        scratch_shapes=[
            pltpu.VMEM((rows, GLA_DK), F32),
            pltpu.VMEM((GLA_HEADS, GLA_HV, GLA_HK), F32),
        ],
        compiler_params=_cparams(("parallel", "arbitrary")),
        name="gla_core",
    )(proj, proj, proj, proj, proj, wg_pad_bf, b_gate.reshape(1, GLA_DK), norm_g.reshape(1, GLA_DV), tri)


def _rg_kernel(y_ref, x_ref, cw_ref, cb_ref, wg_ref, brg_ref, big_ref, lam_ref,
               o_ref, xext_ref, hcar_ref):
    rows = x_ref.shape[0]

    @pl.when(pl.program_id(1) == 0)
    def _():
        xext_ref[0:8, :] = jnp.zeros((8, RG_WIDTH), F32)
        hcar_ref[...] = jnp.zeros_like(hcar_ref)

    xext_ref[8:8 + rows, :] = x_ref[...]
    xc = cb_ref[...] + xext_ref[pl.ds(8 - (RG_CONV - 1), rows), :] * cw_ref[0:1, :]
    for j in range(1, RG_CONV):
        xc = xc + xext_ref[pl.ds(8 - (RG_CONV - 1) + j, rows), :] * cw_ref[j:j + 1, :]
    xext_ref[0:8, :] = x_ref[rows - 8:rows, :]

    r_parts, i_parts = [], []
    for n in range(RG_BLOCKS):
        cs = slice(n * RG_BW, (n + 1) * RG_BW)
        g = jnp.dot(xc[:, cs].astype(BF16), wg_ref[n], preferred_element_type=F32)
        r_parts.append(g[:, :RG_BW])
        i_parts.append(g[:, RG_BW:])
    r = jax.nn.sigmoid(jnp.concatenate(r_parts, axis=-1) + brg_ref[...])
    ig = jax.nn.sigmoid(jnp.concatenate(i_parts, axis=-1) + big_ref[...])

    log_a = -RG_C * r * _softplus(-lam_ref[...])
    a = jnp.exp(log_a)
    th = jnp.tanh(log_a)
    u = jnp.sqrt(-2.0 * th / (1.0 - th)) * (ig * xc)

    ridx = lax.broadcasted_iota(I32, (rows, 1), 0)
    acc_a, acc_h = a, u
    d = 1
    while d < rows:
        keep = ridx >= d
        sh_a = pltpu.roll(acc_a, d, axis=0)
        sh_h = pltpu.roll(acc_h, d, axis=0)
        acc_h = jnp.where(keep, acc_a * sh_h + acc_h, acc_h)
        acc_a = jnp.where(keep, acc_a * sh_a, acc_a)
        d *= 2
    h = acc_h + acc_a * hcar_ref[...]
    hcar_ref[...] = h[rows - 1:rows, :]

    yb = y_ref[...]
    gelu = 0.5 * yb * (1.0 + jnp.tanh(0.7978845608028654 * (yb + 0.044715 * (yb * yb * yb))))
    o_ref[...] = h * gelu


def _rg_call(proj, conv_w, conv_b, wgate_bf, b_rg, b_ig, lam, batch, seq):
    t = proj.shape[0]
    rows = min(RG_ROWS, seq)
    per_seq = seq // rows
    row_map = lambda b, j: b * per_seq + j
    vec = lambda: pl.BlockSpec((1, RG_WIDTH), lambda b, j: (0, 0))
    return pl.pallas_call(
        _rg_kernel,
        grid=(batch, per_seq),
        in_specs=[
            pl.BlockSpec((rows, RG_WIDTH), lambda b, j: (row_map(b, j), 0)),
            pl.BlockSpec((rows, RG_WIDTH), lambda b, j: (row_map(b, j), 1)),
            pl.BlockSpec((RG_CONV, RG_WIDTH), lambda b, j: (0, 0)),
            vec(),
            pl.BlockSpec((RG_BLOCKS, RG_BW, 2 * RG_BW), lambda b, j: (0, 0, 0)),
            vec(), vec(), vec(),
        ],
        out_specs=pl.BlockSpec((rows, RG_WIDTH), lambda b, j: (row_map(b, j), 0)),
        out_shape=jax.ShapeDtypeStruct((t, RG_WIDTH), F32),
        scratch_shapes=[
            pltpu.VMEM((rows + 8, RG_WIDTH), F32),
            pltpu.VMEM((1, RG_WIDTH), F32),
        ],
        compiler_params=_cparams(("parallel", "arbitrary")),
        name="rglru_core",
    )(proj, proj, conv_w, conv_b.reshape(1, RG_WIDTH), wgate_bf, b_rg.reshape(1, RG_WIDTH),
      b_ig.reshape(1, RG_WIDTH), lam.reshape(1, RG_WIDTH))


def _fox_bias_selectors():
    piece = lambda part, head: part * FOX_HEADS + head
    one = 3 * FOX_HEADS
    sel_q = np.zeros((LANES, FOX_PAIRS * LANES), np.float32)
    sel_ka = np.zeros((LANES, FOX_PAIRS * LANES), np.float32)
    sel_kb = np.zeros((LANES, FOX_PAIRS * LANES), np.float32)
    for p in range(FOX_PAIRS):
        base = p * LANES
        for side, head in enumerate((2 * p, 2 * p + 1)):
            sel_k = sel_ka if side == 0 else sel_kb
            for part in range(3):
                sel_q[piece(part, head), base + 6 * side + part] = 1.0
                sel_q[one, base + 6 * side + 3 + part] = 1.0
                sel_k[one, base + 6 * side + part] = 1.0
                sel_k[piece(part, head), base + 6 * side + 3 + part] = -1.0
    return sel_q, sel_ka, sel_kb


def _fox_prep_kernel(q_ref, k_ref, v_ref, f_ref, bf_ref, gq_ref, gk_ref, grp_ref, tri_ref,
                     sq_ref, ska_ref, skb_ref, qa_ref, ka_ref, kb_ref, vb_ref, dcar_ref):
    @pl.when(pl.program_id(1) == 0)
    def _():
        dcar_ref[...] = jnp.zeros_like(dcar_ref)

    lane = lax.broadcasted_iota(I32, (1, LANES), 1)
    log_f = -_softplus(-(f_ref[...] + bf_ref[...]))
    log_f = jnp.where(lane < FOX_HEADS, log_f, 0.0)
    dcum = jnp.dot(tri_ref[...], log_f, preferred_element_type=F32,
                   precision=lax.Precision.HIGHEST) + dcar_ref[...]
    dcar_ref[...] = dcum[dcum.shape[0] - 1:, :]

    d2 = dcum * LOG2E
    d_hi = d2.astype(BF16)
    rem = d2 - d_hi.astype(F32)
    d_mid = rem.astype(BF16)
    d_lo = (rem - d_mid.astype(F32)).astype(BF16)
    hi, mid, lo = (pltpu.roll(p.astype(F32), s, axis=1) for p, s in
                   ((d_hi, 0), (d_mid, FOX_HEADS), (d_lo, 2 * FOX_HEADS)))
    pieces = jnp.where(lane < FOX_HEADS, hi,
                       jnp.where(lane < 2 * FOX_HEADS, mid,
                                 jnp.where(lane < 3 * FOX_HEADS, lo,
                                           jnp.where(lane == 3 * FOX_HEADS, 1.0, 0.0)))).astype(BF16)
    bias_q = jnp.dot(pieces, sq_ref[...], preferred_element_type=F32).astype(BF16)
    bias_ka = jnp.dot(pieces, ska_ref[...], preferred_element_type=F32).astype(BF16)
    bias_kb = jnp.dot(pieces, skb_ref[...], preferred_element_type=F32).astype(BF16)

    grp = grp_ref[...]
    first_head = lane < FOX_HD

    def rms(xt, gain):
        sq = xt * xt
        sq_hi = sq.astype(BF16)
        sq_lo = (sq - sq_hi.astype(F32)).astype(BF16)
        ssum = (jnp.dot(sq_hi, grp, preferred_element_type=F32)
                + jnp.dot(sq_lo, grp, preferred_element_type=F32))
        return xt * lax.rsqrt(ssum * (1.0 / FOX_HD) + LN_EPS) * gain

    for p in range(FOX_PAIRS):
        src = slice(p * LANES, (p + 1) * LANES)
        dst = slice(p * FOX_AUG, p * FOX_AUG + LANES)
        dst_bias = slice(p * FOX_AUG + LANES, (p + 1) * FOX_AUG)
        qn = rms(q_ref[:, src], gq_ref[...] * (FOX_HD ** -0.5 * LOG2E)).astype(BF16)
        kn = rms(k_ref[:, src], gk_ref[...]).astype(BF16)
        qa_ref[:, dst] = qn
        qa_ref[:, dst_bias] = bias_q[:, src]
        ka_ref[:, dst] = jnp.where(first_head, kn, jnp.zeros_like(kn))
        ka_ref[:, dst_bias] = bias_ka[:, src]
        kb_ref[:, dst] = jnp.where(first_head, jnp.zeros_like(kn), kn)
        kb_ref[:, dst_bias] = bias_kb[:, src]
    vb_ref[...] = v_ref[...].astype(BF16)


def _fox_prep_call(proj, b_f, q_norm_g, k_norm_g, batch, seq):
    t = proj.shape[0]
    d = FOX_HEADS * FOX_HD
    rows = min(FOX_PREP_ROWS, seq)
    per_seq = seq // rows
    row_map = lambda b, j: b * per_seq + j
    sel_q, sel_ka, sel_kb = (jnp.asarray(s, BF16) for s in _fox_bias_selectors())
    grp = np.kron(np.eye(2, dtype=np.float32), np.ones((FOX_HD, FOX_HD), np.float32))
    tri = np.tril(np.ones((rows, rows), np.float32))
    bf_pad = jnp.zeros((1, LANES), F32).at[0, :FOX_HEADS].set(b_f)
    gq2 = jnp.tile(q_norm_g, 2).reshape(1, LANES)
    gk2 = jnp.tile(k_norm_g, 2).reshape(1, LANES)
    const = lambda shape: pl.BlockSpec(shape, lambda b, j: tuple(0 for _ in shape))
    aug = jax.ShapeDtypeStruct((t, FOX_PAIRS * FOX_AUG), BF16)
    aug_spec = pl.BlockSpec((rows, FOX_PAIRS * FOX_AUG), lambda b, j: (row_map(b, j), 0))
    return pl.pallas_call(
        _fox_prep_kernel,
        grid=(batch, per_seq),
        in_specs=[
            pl.BlockSpec((rows, d), lambda b, j: (row_map(b, j), 0)),
            pl.BlockSpec((rows, d), lambda b, j: (row_map(b, j), 1)),
            pl.BlockSpec((rows, d), lambda b, j: (row_map(b, j), 2)),
            pl.BlockSpec((rows, LANES), lambda b, j: (row_map(b, j), 4 * d // LANES)),
            const((1, LANES)), const((1, LANES)), const((1, LANES)),
            const((LANES, LANES)), const((rows, rows)),
            const((LANES, FOX_PAIRS * LANES)), const((LANES, FOX_PAIRS * LANES)),
            const((LANES, FOX_PAIRS * LANES)),
        ],
        out_specs=[aug_spec, aug_spec, aug_spec,
                   pl.BlockSpec((rows, d), lambda b, j: (row_map(b, j), 0))],
        out_shape=[aug, aug, aug, jax.ShapeDtypeStruct((t, d), BF16)],
        scratch_shapes=[pltpu.VMEM((1, LANES), F32)],
        compiler_params=_cparams(("parallel", "arbitrary")),
        name="fox_prep",
    )(proj, proj, proj, proj, bf_pad, gq2, gk2, jnp.asarray(grp, BF16), jnp.asarray(tri),
      sel_q, sel_ka, sel_kb)


def _fox_attn_kernel(qa_ref, ka_ref, kb_ref, v_ref, og_ref, o_ref, m_ref, l_ref, acc_ref):
    qi = pl.program_id(2)
    tq = qa_ref.shape[0]
    sub = lax.broadcasted_iota(I32, (LANES, 1), 0)
    first_head = sub < FOX_HD

    m_ref[...] = jnp.full_like(m_ref, NEG_BIG)
    l_ref[...] = jnp.zeros_like(l_ref)
    acc_ref[...] = jnp.zeros_like(acc_ref)

    def steps(blocks):
        q = qa_ref[...]
        scored = []
        for ki, masked in blocks:
            rows = pl.ds(pl.multiple_of(ki * tq, tq), tq)
            scored.append((rows, masked, [
                lax.dot_general(k_ref[rows, :], q, (((1,), (1,)), ((), ())), preferred_element_type=F32)
                for k_ref in (ka_ref, kb_ref)]))
        for rows, masked, sts in scored:
            v = v_ref[rows, :]
            pv, alpha = [], []
            for side, st in enumerate(sts):
                if masked:
                    key = lax.broadcasted_iota(I32, st.shape, 0)
                    qry = lax.broadcasted_iota(I32, st.shape, 1)
                    st = jnp.where(key <= qry, st, NEG_BIG)
                m_prev = m_ref[side:side + 1, :]
                m_new = jnp.maximum(m_prev, jnp.max(st, axis=0, keepdims=True))
                a = jnp.exp2(m_prev - m_new)
                p = jnp.exp2(st - m_new)
                l_ref[side:side + 1, :] = a * l_ref[side:side + 1, :] + jnp.sum(p, axis=0, keepdims=True)
                m_ref[side:side + 1, :] = m_new
                pv.append(lax.dot_general(v, p.astype(BF16), (((0,), (0,)), ((), ())),
                                          preferred_element_type=F32))
                alpha.append(a)
            acc_ref[...] = (acc_ref[...] * jnp.where(first_head, alpha[0], alpha[1])
                            + jnp.where(first_head, pv[0], pv[1]))

    def body(j, carry):
        steps([(2 * j, False), (2 * j + 1, False)])
        return carry

    lax.fori_loop(0, qi // 2, body, 0)

    @pl.when(qi % 2 == 1)
    def _():
        steps([(qi - 1, False), (qi, True)])

    @pl.when(qi % 2 == 0)
    def _():
        steps([(qi, True)])

    out_t = acc_ref[...] / jnp.where(first_head, l_ref[0:1, :], l_ref[1:2, :])
    o_ref[...] = out_t.T * jax.nn.sigmoid(og_ref[...])


def _fox_attn_call(qa, ka, kb, vb, proj, batch, seq):
    t = qa.shape[0]
    d = FOX_HEADS * FOX_HD
    tq = min(FOX_TQ, seq)
    nq = seq // tq
    q_map = lambda b, p, i: (b * nq + i, p)
    kv_map = lambda b, p, i: (b, p)
    return pl.pallas_call(
        _fox_attn_kernel,
        grid=(batch, FOX_PAIRS, nq),
        in_specs=[
            pl.BlockSpec((tq, FOX_AUG), q_map),
            pl.BlockSpec((seq, FOX_AUG), kv_map),
            pl.BlockSpec((seq, FOX_AUG), kv_map),
            pl.BlockSpec((seq, LANES), kv_map),
            pl.BlockSpec((tq, LANES), lambda b, p, i: (b * nq + i, 3 * d // LANES + p)),
        ],
        out_specs=pl.BlockSpec((tq, LANES), q_map),
        out_shape=jax.ShapeDtypeStruct((t, d), F32),
        scratch_shapes=[
            pltpu.VMEM((2, tq), F32),
            pltpu.VMEM((2, tq), F32),
            pltpu.VMEM((LANES, tq), F32),
        ],
        compiler_params=_cparams(("parallel", "parallel", "arbitrary")),
        name="fox_attn",
    )(qa, ka, kb, vb, proj)


def _router_kernel(x_ref, sc_ref, sh_ref, whi_ref, wlo_ref, b_ref, h_ref, idx_ref, wt_ref, cnt_ref):
    h = x_ref[...] * (1.0 + sc_ref[0]) + sh_ref[0]
    h_ref[:, 0, :] = h
    h_hi = h.astype(BF16)
    h_lo = (h - h_hi.astype(F32)).astype(BF16)
    logits = (jnp.dot(h_hi, whi_ref[...], preferred_element_type=F32)
              + jnp.dot(h_lo, whi_ref[...], preferred_element_type=F32)
              + jnp.dot(h_hi, wlo_ref[...], preferred_element_type=F32)
              + b_ref[...])
    lane = lax.broadcasted_iota(I32, logits.shape, 1)
    logits = jnp.where(lane < N_EXPERTS, logits, NEG_BIG)
    idx_out = jnp.zeros(logits.shape, I32)
    wt_out = jnp.zeros(logits.shape, F32)
    picked = jnp.zeros(logits.shape, I32)
    top0 = None
    denom = None
    for k in range(TOP_K):
        m = jnp.max(logits, axis=-1, keepdims=True)
        sel = jnp.min(jnp.where(logits == m, lane, LANES), axis=-1, keepdims=True)
        hit = lane == sel
        picked = picked + hit.astype(I32)
        logits = jnp.where(hit, NEG_BIG, logits)
        if k == 0:
            top0 = m
        e = jnp.exp(m - top0)
        denom = e if k == 0 else denom + e
        idx_out = jnp.where(lane == k, sel, idx_out)
        wt_out = jnp.where(lane == k, e, wt_out)
    idx_ref[...] = idx_out
    wt_ref[...] = wt_out / denom
    cnt_ref[0] = jnp.sum(picked, axis=0, keepdims=True)


def _router_call(x, sc, sh, w_hi, w_lo, b_pad, seq, tm):
    t, d = x.shape
    per_seq = seq // tm
    const = lambda shape: pl.BlockSpec(shape, lambda i: tuple(0 for _ in shape))
    return pl.pallas_call(
        _router_kernel,
        grid=(t // tm,),
        in_specs=[
            pl.BlockSpec((tm, d), lambda i: (i, 0)),
            pl.BlockSpec((1, 1, d), lambda i: (i // per_seq, 0, 0)),
            pl.BlockSpec((1, 1, d), lambda i: (i // per_seq, 0, 0)),
            const((d, LANES)), const((d, LANES)), const((1, LANES)),
        ],
        out_specs=[
            pl.BlockSpec((tm, 1, d), lambda i: (i, 0, 0)),
            pl.BlockSpec((tm, LANES), lambda i: (i, 0)),
            pl.BlockSpec((tm, LANES), lambda i: (i, 0)),
            pl.BlockSpec((1, 1, LANES), lambda i: (i, 0, 0)),
        ],
        out_shape=[
            jax.ShapeDtypeStruct((t, 1, d), F32),
            jax.ShapeDtypeStruct((t, LANES), I32),
            jax.ShapeDtypeStruct((t, LANES), F32),
            jax.ShapeDtypeStruct((t // tm, 1, LANES), I32),
        ],
        compiler_params=_cparams(("parallel",)),
        name="moe_router",
    )(x, sc, sh, w_hi, w_lo, b_pad)


MOE_GROUP = 2 * LANES


def _moe_kernel(fb_ref, nb_ref, src_ref, dst_ref, nact_ref,
                h_hbm, wgu_ref, bgu_ref, wdn_ref, bdn_ref, perm_ref,
                y_hbm,
                xa, xb, xc, ya, yb, yc, wgu_bf, wdn_bf, gsem, ssem):
    e = pl.program_id(0)
    n_act = nact_ref[0]
    rows = MOE_ROWS
    n_groups = wgu_ref.shape[3] // MOE_GROUP
    xbufs, ybufs = (xa, xb, xc), (ya, yb, yc)
    n_buf = len(xbufs)

    def gather_start(blk, r, buf):
        tok = src_ref[blk * rows + r]
        pltpu.make_async_copy(h_hbm.at[tok], xbufs[buf].at[pl.ds(r, 1), :], gsem.at[buf]).start()

    def scatter_start(blk_ext, r, buf):
        row = dst_ref[blk_ext * rows + r]
        pltpu.make_async_copy(ybufs[buf].at[pl.ds(r, 1), :], y_hbm.at[row], ssem.at[buf]).start(priority=1)

    def wait_gather(buf):
        pltpu.make_async_copy(xbufs[buf], xbufs[buf], gsem.at[buf]).wait()

    def wait_scatter(buf):
        pltpu.make_async_copy(ybufs[buf], ybufs[buf], ssem.at[buf]).wait()

    def rolled(start_fn, blk, buf):
        def body(r, c):
            start_fn(blk, r, buf)
            return c
        lax.fori_loop(0, rows, body, 0)

    @pl.when(e == 0)
    def _():
        rolled(gather_start, 0, 0)
        rolled(gather_start, 1, 1)
        for buf in range(n_buf):
            ybufs[buf][...] = jnp.zeros_like(ybufs[buf])
        for par in range(2):
            def zero_row(r, c, par=par):
                row = dst_ref[r] + (par - 1) * rows
                pltpu.make_async_copy(ybufs[par].at[pl.ds(r, 1), :], y_hbm.at[row], ssem.at[par]).start()
                return c
            lax.fori_loop(0, rows, zero_row, 0)
            wait_scatter(par)

    def block_step(cur, i):
        prev = (cur + 2) % n_buf
        prev2 = (cur + 1) % n_buf
        wait_gather(cur)
        x = xbufs[cur][...].astype(BF16)
        gu = jnp.dot(x, wgu_bf[...], preferred_element_type=F32) + bgu_ref[0, 0]
        parts = []
        for g in range(n_groups):
            gate = jnp.minimum(gu[:, g * MOE_GROUP:g * MOE_GROUP + LANES], SWIGLU_LIMIT)
            up = jnp.clip(gu[:, g * MOE_GROUP + LANES:(g + 1) * MOE_GROUP], -SWIGLU_LIMIT, SWIGLU_LIMIT)
            glu = gate * jax.nn.sigmoid(SWIGLU_ALPHA * gate)
            parts.append(((up + 1.0) * glu).astype(BF16))
        hmid = jnp.concatenate(parts, axis=-1)
        ybufs[cur][...] = jnp.dot(hmid, wdn_bf[...], preferred_element_type=F32) + bdn_ref[0, 0]
        for r in range(rows):
            gather_start(i + 2, r, prev)
        for r in range(rows):
            scatter_start(i, r, prev)

        @pl.when(i >= 1)
        def _():
            wait_scatter(prev2)

        @pl.when(i == n_act - 1)
        def _():
            rolled(scatter_start, i + 1, cur)
            wait_scatter(prev)
            wait_scatter(cur)
            wait_gather(prev2)
            wait_gather(prev)

    perm = perm_ref[...]
    for g in range(n_groups):
        cs = slice(g * MOE_GROUP, (g + 1) * MOE_GROUP)
        wgu_bf[:, cs] = jnp.dot(wgu_ref[0, 0, :, cs].astype(BF16), perm,
                                preferred_element_type=F32).astype(BF16)
    wdn_bf[...] = wdn_ref[0, 0].astype(BF16)

    def expert_block(j, carry):
        i = fb_ref[e] + j
        for buf in range(n_buf):
            @pl.when(i % n_buf == buf)
            def _(buf=buf):
                block_step(buf, i)
        return carry

    lax.fori_loop(0, nb_ref[e], expert_block, 0)


def _moe_call(h, first_blk, n_blk, src, dst, nact, w_gu, b_gu, w_down, b_down, layer, n_out_rows):
    d = h.shape[-1]
    _, n_exp, _, f2 = w_gu.shape
    f = w_down.shape[2]
    perm = np.zeros((MOE_GROUP, MOE_GROUP), np.float32)
    for jj in range(LANES):
        perm[2 * jj, jj] = 1.0
        perm[2 * jj + 1, LANES + jj] = 1.0
    b_gu_grouped = b_gu[layer].reshape(n_exp, f2 // MOE_GROUP, LANES, 2).transpose(0, 1, 3, 2).reshape(
        1, n_exp, 1, f2)
    by_expert = lambda e, *_: (layer, e, 0, 0)
    grid_spec = pltpu.PrefetchScalarGridSpec(
        num_scalar_prefetch=5,
        grid=(n_exp,),
        in_specs=[
            pl.BlockSpec(memory_space=pl.ANY),
            pl.BlockSpec((1, 1, d, f2), by_expert),
            pl.BlockSpec((1, 1, 1, f2), lambda e, *_: (0, e, 0, 0)),
            pl.BlockSpec((1, 1, f, d), by_expert),
            pl.BlockSpec((1, 1, 1, d), by_expert),
            pl.BlockSpec((MOE_GROUP, MOE_GROUP), lambda e, *_: (0, 0)),
        ],
        out_specs=pl.BlockSpec(memory_space=pl.ANY),
        scratch_shapes=[
            *[pltpu.VMEM((MOE_ROWS, d), F32) for _ in range(6)],
            pltpu.VMEM((d, f2), BF16),
            pltpu.VMEM((f, d), BF16),
            pltpu.SemaphoreType.DMA((3,)),
            pltpu.SemaphoreType.DMA((3,)),
        ],
    )
    return pl.pallas_call(
        _moe_kernel,
        grid_spec=grid_spec,
        out_shape=jax.ShapeDtypeStruct((n_out_rows, 1, d), F32),
        compiler_params=_cparams(("arbitrary",)),
        name="moe_experts",
    )(first_blk, n_blk, src, dst, nact, h, w_gu, b_gu_grouped, w_down, b_down.reshape(b_down.shape[0], n_exp, 1, d),
      jnp.asarray(perm, BF16))


def _combine_kernel(y0_ref, y1_ref, y2_ref, y3_ref, wt_ref, x_ref, g_ref, lg_ref, lb_ref, o_ref):
    wt = wt_ref[...]
    y = y0_ref[:, 0, :] * wt[:, 0:1]
    for k, y_ref in enumerate((y1_ref, y2_ref, y3_ref), start=1):
        y = y + y_ref[:, 0, :] * wt[:, k:k + 1]
    z = ALPHA * x_ref[...] + (1.0 + g_ref[0]) * y
    o_ref[...] = _layer_norm_rows(z, lg_ref[...], lb_ref[...])


def _combine_call(y4, wt, x, gate, ln_g, ln_b, seq, tm):
    t, d = x.shape
    per_seq = seq // tm
    nblk = t // tm
    y_spec = lambda k: pl.BlockSpec((tm, 1, d), lambda i, k=k: (k * nblk + i, 0, 0))
    return pl.pallas_call(
        _combine_kernel,
        grid=(nblk,),
        in_specs=[
            y_spec(0), y_spec(1), y_spec(2), y_spec(3),
            pl.BlockSpec((tm, LANES), lambda i: (i, 0)),
            pl.BlockSpec((tm, d), lambda i: (i, 0)),
            pl.BlockSpec((1, 1, d), lambda i: (i // per_seq, 0, 0)),
            pl.BlockSpec((1, d), lambda i: (0, 0)),
            pl.BlockSpec((1, d), lambda i: (0, 0)),
        ],
        out_specs=pl.BlockSpec((tm, d), lambda i: (i, 0)),
        out_shape=jax.ShapeDtypeStruct((t, d), F32),
        compiler_params=_cparams(("parallel",)),
        name="moe_combine_ln",
    )(y4, y4, y4, y4, wt, x, gate, ln_g.reshape(1, d), ln_b.reshape(1, d))


def _moe_plan(top_idx, block_counts, t):
    n_assign = t * TOP_K
    n_blocks = -(-(n_assign + N_EXPERTS * (MOE_ROWS - 1)) // MOE_ROWS)
    experts = jnp.arange(N_EXPERTS, dtype=I32)
    order = jnp.argsort(top_idx.reshape(-1)).astype(I32)
    counts = jnp.sum(block_counts[:, 0, :N_EXPERTS], axis=0)
    starts = jnp.cumsum(counts) - counts
    padded = (counts + MOE_ROWS - 1) // MOE_ROWS * MOE_ROWS
    pad_ends = jnp.cumsum(padded)
    pad_starts = pad_ends - padded
    blocks = jnp.arange(n_blocks + 2, dtype=I32)
    first_row = blocks * MOE_ROWS
    block_e = jnp.minimum(jnp.sum((pad_ends[None, :] <= first_row[:, None]).astype(I32), axis=1), N_EXPERTS - 1)
    of_block = lambda v: jnp.sum(jnp.where(block_e[:, None] == experts[None, :], v[None, :], 0), axis=1)
    seg_off = first_row - of_block(pad_starts)
    n_valid = jnp.where(first_row < pad_ends[-1], jnp.clip(of_block(counts) - seg_off, 0, MOE_ROWS), 0)
    r = jnp.arange(MOE_ROWS, dtype=I32)
    valid = r[None, :] < n_valid[:, None]
    assign = order[jnp.clip((of_block(starts) + seg_off)[:, None] + r[None, :], 0, n_assign - 1)]
    dump = n_assign + (blocks[:, None] % 2) * MOE_ROWS + r[None, :]
    src = jnp.where(valid, assign // TOP_K, 0).reshape(-1)
    dst = jnp.where(valid, (assign % TOP_K) * t + assign // TOP_K, dump)
    dst_ext = jnp.concatenate([dump[1], dst[:n_blocks].reshape(-1)])
    n_act = (pad_ends[-1] // MOE_ROWS).astype(I32).reshape(1)
    return pad_starts // MOE_ROWS, padded // MOE_ROWS, src, dst_ext, n_act, n_assign + 2 * MOE_ROWS


def _moe_layer(x, sc, sh, gate, ln_g, ln_b, w_router, b_router, w_gu, b_gu, w_down, b_down, layer, seq, tm):
    t, d = x.shape
    w_pad = jnp.zeros((d, LANES), F32).at[:, :N_EXPERTS].set(w_router)
    w_hi = w_pad.astype(BF16)
    w_lo = (w_pad - w_hi.astype(F32)).astype(BF16)
    b_pad = jnp.zeros((1, LANES), F32).at[0, :N_EXPERTS].set(b_router)
    h, idx, wt, block_counts = _router_call(x, sc, sh, w_hi, w_lo, b_pad, seq, tm)
    first_blk, n_blk, src, dst, n_act, n_rows = _moe_plan(idx[:, :TOP_K], block_counts, t)
    y4 = _moe_call(h, first_blk, n_blk, src, dst, n_act, w_gu, b_gu, w_down, b_down, layer, n_rows)
    return _combine_call(y4, wt, x, gate, ln_g, ln_b, seq, tm)


def _pad_cols(w, n):
    return jnp.pad(w, ((0, 0), (0, n - w.shape[1])))


def kernel(x, c, ada_w, ada_b, ln_g, ln_b, gla_w_in, gla_w_gate_up, gla_b_gate, gla_norm_g, gla_w_out, rg_w_in, rg_conv_w, rg_conv_b, rg_w_rg, rg_b_rg, rg_w_ig, rg_b_ig, rg_lambda, rg_w_out, fox_w_in, fox_b_f, fox_q_norm_g, fox_k_norm_g, fox_w_out, moe_w_router, moe_b_router, moe_w_gu, moe_b_gu, moe_w_down, moe_b_down):
    batch, seq, d = x.shape
    t = batch * seq
    tm = min(512, seq)
    depth = ada_w.shape[0]
    xt = x.reshape(t, d)
    mod = _ada_call(c, ada_w, ada_b)

    for l in range(depth):
        part = lambda i: mod[l, :, i * d:(i + 1) * d].reshape(batch, 1, d)
        sh1, sc1, g1, sh2, sc2, g2 = (part(i) for i in range(6))
        kind, j = l % 3, l // 3
        if kind == 0:
            n_in = 2 * GLA_DK + 2 * GLA_DV + LANES
            proj = _modmm_call(xt, sc1, sh1, _pad_cols(gla_w_in[j], n_in).astype(BF16), seq, tm)
            wg = jnp.pad(gla_w_gate_up[j], ((0, LANES - GLA_RANK), (0, 0))).astype(BF16)
            y = _gla_call(proj, wg, gla_b_gate[j], gla_norm_g[j], batch, seq)
            w_out = gla_w_out[j]
        elif kind == 1:
            proj = _modmm_call(xt, sc1, sh1, rg_w_in[j].astype(BF16), seq, tm)
            wgate = jnp.concatenate([rg_w_rg[j], rg_w_ig[j]], axis=-1).astype(BF16)
            y = _rg_call(proj, rg_conv_w[j], rg_conv_b[j], wgate, rg_b_rg[j], rg_b_ig[j],
                         rg_lambda[j], batch, seq)
            w_out = rg_w_out[j]
        else:
            w = fox_w_in[j]
            w = jnp.concatenate([w[:, :3 * d], w[:, 3 * d + FOX_HEADS:], w[:, 3 * d:3 * d + FOX_HEADS]], axis=1)
            proj = _modmm_call(xt, sc1, sh1, _pad_cols(w, 4 * d + LANES).astype(BF16), seq, tm)
            qa, ka, kb, vb = _fox_prep_call(proj, fox_b_f[j], fox_q_norm_g[j], fox_k_norm_g[j], batch, seq)
            y = _fox_attn_call(qa, ka, kb, vb, proj, batch, seq)
            w_out = fox_w_out[j]
        xt = _outln_call(y, w_out.astype(BF16), xt, g1, ln_g[l, 0], ln_b[l, 0], seq, tm)
        xt = _moe_layer(xt, sc2, sh2, g2, ln_g[l, 1], ln_b[l, 1], moe_w_router[l], moe_b_router[l],
                        moe_w_gu, moe_b_gu, moe_w_down, moe_b_down, l, seq, tm)
    return xt.reshape(batch, seq, d)
```

```python
import functools

import numpy as np
import jax
import jax.numpy as jnp
from jax import lax
from jax.experimental import pallas as pl
from jax.experimental.pallas import tpu as pltpu

F32 = jnp.float32
BF16 = jnp.bfloat16
I32 = jnp.int32

DEPTH = 4
ALPHA = (2.0 * DEPTH) ** 0.25
LN_EPS = 1e-5

GLA_HEADS = 4
GLA_HK = 128
GLA_HV = 256
GLA_DK = GLA_HEADS * GLA_HK
GLA_DV = GLA_HEADS * GLA_HV
GLA_RANK = 16
GLA_TAU = 16.0
GLA_CHUNK = 64
GLA_ROWS = 512

RG_BW = 128
RG_BLOCKS = 10
RG_WIDTH = RG_BW * RG_BLOCKS
RG_CONV = 4
RG_C = 8.0
RG_ROWS = 256

FOX_HD = 64
FOX_HEADS = 16
FOX_PAIRS = FOX_HEADS // 2
FOX_TQ = 512
FOX_PREP_ROWS = 256
FOX_AUG = 256

N_EXPERTS = 32
TOP_K = 4
SWIGLU_LIMIT = 7.0
SWIGLU_ALPHA = 1.702
MOE_ROWS = 256
NEG_BIG = -1e30
LOG2E = 1.4426950408889634

LANES = 128
VMEM_LIMIT = 56 * 1024 * 1024


def _cparams(sem, vmem=VMEM_LIMIT):
    return pltpu.CompilerParams(dimension_semantics=sem, vmem_limit_bytes=vmem)


def _softplus(z):
    return jnp.maximum(z, 0.0) + jnp.log(1.0 + jnp.exp(-jnp.abs(z)))


def _layer_norm_rows(z, g, b):
    mu = jnp.mean(z, axis=-1, keepdims=True)
    zc = z - mu
    var = jnp.mean(zc * zc, axis=-1, keepdims=True)
    return zc * lax.rsqrt(var + LN_EPS) * g + b


def _ada_kernel(c_ref, w_ref, b_ref, o_ref):
    c = c_ref[...]
    ca = c * jax.nn.sigmoid(c)
    o_ref[0] = jnp.dot(ca.astype(BF16), w_ref[0].astype(BF16), preferred_element_type=F32) + b_ref[0]


def _ada_call(c, ada_w, ada_b):
    depth, d, n = ada_w.shape
    b = c.shape[0]
    tn = n // 4
    return pl.pallas_call(
        _ada_kernel,
        grid=(depth, n // tn),
        in_specs=[
            pl.BlockSpec((b, d), lambda l, j: (0, 0)),
            pl.BlockSpec((1, d, tn), lambda l, j: (l, 0, j)),
            pl.BlockSpec((1, 1, tn), lambda l, j: (l, 0, j)),
        ],
        out_specs=pl.BlockSpec((1, b, tn), lambda l, j: (l, 0, j)),
        out_shape=jax.ShapeDtypeStruct((depth, b, n), F32),
        compiler_params=_cparams(("parallel", "parallel")),
        name="ada_mod",
    )(c, ada_w, ada_b.reshape(depth, 1, n))


def _modmm_kernel(x_ref, sc_ref, sh_ref, w_ref, o_ref):
    h = x_ref[...] * (1.0 + sc_ref[0]) + sh_ref[0]
    o_ref[...] = jnp.dot(h.astype(BF16), w_ref[...], preferred_element_type=F32)


def _modmm_call(x, sc, sh, w_bf, seq, tm):
    t, d = x.shape
    n = w_bf.shape[1]
    per_seq = seq // tm
    return pl.pallas_call(
        _modmm_kernel,
        grid=(t // tm,),
        in_specs=[
            pl.BlockSpec((tm, d), lambda i: (i, 0)),
            pl.BlockSpec((1, 1, d), lambda i: (i // per_seq, 0, 0)),
            pl.BlockSpec((1, 1, d), lambda i: (i // per_seq, 0, 0)),
            pl.BlockSpec((d, n), lambda i: (0, 0)),
        ],
        out_specs=pl.BlockSpec((tm, n), lambda i: (i, 0)),
        out_shape=jax.ShapeDtypeStruct((t, n), F32),
        compiler_params=_cparams(("parallel",)),
        name="mod_inproj",
    )(x, sc, sh, w_bf)


def _outln_kernel(y_ref, w_ref, x_ref, g_ref, lg_ref, lb_ref, o_ref):
    y = jnp.dot(y_ref[...].astype(BF16), w_ref[...], preferred_element_type=F32)
    z = ALPHA * x_ref[...] + (1.0 + g_ref[0]) * y
    o_ref[...] = _layer_norm_rows(z, lg_ref[...], lb_ref[...])


def _outln_call(y, w_bf, x, gate, ln_g, ln_b, seq, tm):
    t, k = y.shape
    d = x.shape[1]
    per_seq = seq // tm
    return pl.pallas_call(
        _outln_kernel,
        grid=(t // tm,),
        in_specs=[
            pl.BlockSpec((tm, k), lambda i: (i, 0)),
            pl.BlockSpec((k, d), lambda i: (0, 0)),
            pl.BlockSpec((tm, d), lambda i: (i, 0)),
            pl.BlockSpec((1, 1, d), lambda i: (i // per_seq, 0, 0)),
            pl.BlockSpec((1, d), lambda i: (0, 0)),
            pl.BlockSpec((1, d), lambda i: (0, 0)),
        ],
        out_specs=pl.BlockSpec((tm, d), lambda i: (i, 0)),
        out_shape=jax.ShapeDtypeStruct((t, d), F32),
        compiler_params=_cparams(("parallel",)),
        name="outproj_ln",
    )(y, w_bf, x, gate, ln_g.reshape(1, d), ln_b.reshape(1, d))


def _gla_kernel(q_ref, k_ref, v_ref, r_ref, g_ref, wg_ref, bg_ref, ng_ref, tri_ref,
                o_ref, la_ref, st_ref):
    c_rows = GLA_CHUNK
    half = c_rows // 2

    @pl.when(pl.program_id(1) == 0)
    def _():
        st_ref[...] = jnp.zeros_like(st_ref)

    z = jnp.dot(g_ref[...].astype(BF16), wg_ref[...], preferred_element_type=F32) + bg_ref[...]
    la_ref[...] = -_softplus(-z) * (1.0 / GLA_TAU)

    row = lax.broadcasted_iota(I32, (c_rows, c_rows), 0)
    col = lax.broadcasted_iota(I32, (c_rows, c_rows), 1)
    causal = col <= row
    tri = tri_ref[...]
    n_chunks = q_ref.shape[0] // c_rows

    def chunk(c):
        rows = pl.ds(c * c_rows, c_rows)
        bc = jnp.dot(tri, la_ref[rows, :], preferred_element_type=F32, precision=lax.Precision.HIGHEST)
        b_mid = bc[half - 1:half, :]
        b_last = bc[c_rows - 1:c_rows, :]
        e_q_mid = jnp.exp(bc - b_mid)
        e_k_mid = jnp.exp(b_mid - bc)
        e_q = jnp.exp(bc)
        e_k_end = jnp.exp(b_last - bc)
        dec = jnp.exp(b_last)
        qc = q_ref[rows, :] * (GLA_HK ** -0.5)
        kc = k_ref[rows, :]
        for h in range(GLA_HEADS):
            ks = slice(h * GLA_HK, (h + 1) * GLA_HK)
            vs = slice(h * GLA_HV, (h + 1) * GLA_HV)
            qm = (qc[:, ks] * e_q_mid[:, ks]).astype(BF16)
            km = (kc[:, ks] * e_k_mid[:, ks]).astype(BF16)
            s = lax.dot_general(qm, km, (((1,), (1,)), ((), ())), preferred_element_type=F32)
            s = jnp.where(causal, s, 0.0)
            vh = v_ref[rows, vs].astype(BF16)
            o = jnp.dot(s.astype(BF16), vh, preferred_element_type=F32)
            qd = (qc[:, ks] * e_q[:, ks]).astype(BF16)
            st = st_ref[h]
            o = o + lax.dot_general(qd, st.astype(BF16), (((1,), (1,)), ((), ())),
                                    preferred_element_type=F32)
            ke = (kc[:, ks] * e_k_end[:, ks]).astype(BF16)
            st_ref[h] = st * dec[:, ks] + lax.dot_general(
                vh, ke, (((0,), (0,)), ((), ())), preferred_element_type=F32)
            ms = jnp.mean(o * o, axis=-1, keepdims=True)
            y = o * lax.rsqrt(ms + LN_EPS) * ng_ref[:, vs]
            rr = r_ref[rows, vs]
            o_ref[rows, vs] = y * (rr * jax.nn.sigmoid(rr))

    for c in range(n_chunks):
        chunk(c)


def _gla_call(proj, wg_pad_bf, b_gate, norm_g, batch, seq):
    t = proj.shape[0]
    rows = min(GLA_ROWS, seq)
    per_seq = seq // rows
    tri = jnp.asarray(np.tril(np.ones((GLA_CHUNK, GLA_CHUNK), np.float32)))
    row_map = lambda b, j: b * per_seq + j
    return pl.pallas_call(
        _gla_kernel,
        grid=(batch, per_seq),
        in_specs=[
            pl.BlockSpec((rows, GLA_DK), lambda b, j: (row_map(b, j), 0)),
            pl.BlockSpec((rows, GLA_DK), lambda b, j: (row_map(b, j), 1)),
            pl.BlockSpec((rows, GLA_DV), lambda b, j: (row_map(b, j), 1)),
            pl.BlockSpec((rows, GLA_DV), lambda b, j: (row_map(b, j), 2)),
            pl.BlockSpec((rows, LANES), lambda b, j: (row_map(b, j), (2 * GLA_DK + 2 * GLA_DV) // LANES)),
            pl.BlockSpec((LANES, GLA_DK), lambda b, j: (0, 0)),
            pl.BlockSpec((1, GLA_DK), lambda b, j: (0, 0)),
            pl.BlockSpec((1, GLA_DV), lambda b, j: (0, 0)),
            pl.BlockSpec((GLA_CHUNK, GLA_CHUNK), lambda b, j: (0, 0)),
        ],
        out_specs=pl.BlockSpec((rows, GLA_DV), lambda b, j: (row_map(b, j), 0)),
        out_shape=jax.ShapeDtypeStruct((t, GLA_DV), F32),
        scratch_shapes=[
            pltpu.VMEM((rows, GLA_DK), F32),
            pltpu.VMEM((GLA_HEADS, GLA_HV, GLA_HK), F32),
        ],
        compiler_params=_cparams(("parallel", "arbitrary")),
        name="gla_core",
    )(proj, proj, proj, proj, proj, wg_pad_bf, b_gate.reshape(1, GLA_DK), norm_g.reshape(1, GLA_DV), tri)


def _rg_kernel(y_ref, x_ref, cw_ref, cb_ref, wg_ref, brg_ref, big_ref, lam_ref,
               o_ref, xext_ref, hcar_ref):
    rows = x_ref.shape[0]

    @pl.when(pl.program_id(1) == 0)
    def _():
        xext_ref[0:8, :] = jnp.zeros((8, RG_WIDTH), F32)
        hcar_ref[...] = jnp.zeros_like(hcar_ref)

    xext_ref[8:8 + rows, :] = x_ref[...]
    xc = cb_ref[...] + xext_ref[pl.ds(8 - (RG_CONV - 1), rows), :] * cw_ref[0:1, :]
    for j in range(1, RG_CONV):
        xc = xc + xext_ref[pl.ds(8 - (RG_CONV - 1) + j, rows), :] * cw_ref[j:j + 1, :]
    xext_ref[0:8, :] = x_ref[rows - 8:rows, :]

    r_parts, i_parts = [], []
    for n in range(RG_BLOCKS):
        cs = slice(n * RG_BW, (n + 1) * RG_BW)
        g = jnp.dot(xc[:, cs].astype(BF16), wg_ref[n], preferred_element_type=F32)
        r_parts.append(g[:, :RG_BW])
        i_parts.append(g[:, RG_BW:])
    r = jax.nn.sigmoid(jnp.concatenate(r_parts, axis=-1) + brg_ref[...])
    ig = jax.nn.sigmoid(jnp.concatenate(i_parts, axis=-1) + big_ref[...])

    log_a = -RG_C * r * _softplus(-lam_ref[...])
    a = jnp.exp(log_a)
    th = jnp.tanh(log_a)
    u = jnp.sqrt(-2.0 * th / (1.0 - th)) * (ig * xc)

    ridx = lax.broadcasted_iota(I32, (rows, 1), 0)
    acc_a, acc_h = a, u
    d = 1
    while d < rows:
        keep = ridx >= d
        sh_a = pltpu.roll(acc_a, d, axis=0)
        sh_h = pltpu.roll(acc_h, d, axis=0)
        acc_h = jnp.where(keep, acc_a * sh_h + acc_h, acc_h)
        acc_a = jnp.where(keep, acc_a * sh_a, acc_a)
        d *= 2
    h = acc_h + acc_a * hcar_ref[...]
    hcar_ref[...] = h[rows - 1:rows, :]

    yb = y_ref[...]
    gelu = 0.5 * yb * (1.0 + jnp.tanh(0.7978845608028654 * (yb + 0.044715 * (yb * yb * yb))))
    o_ref[...] = h * gelu


def _rg_call(proj, conv_w, conv_b, wgate_bf, b_rg, b_ig, lam, batch, seq):
    t = proj.shape[0]
    rows = min(RG_ROWS, seq)
    per_seq = seq // rows
    row_map = lambda b, j: b * per_seq + j
    vec = lambda: pl.BlockSpec((1, RG_WIDTH), lambda b, j: (0, 0))
    return pl.pallas_call(
        _rg_kernel,
        grid=(batch, per_seq),
        in_specs=[
            pl.BlockSpec((rows, RG_WIDTH), lambda b, j: (row_map(b, j), 0)),
            pl.BlockSpec((rows, RG_WIDTH), lambda b, j: (row_map(b, j), 1)),
            pl.BlockSpec((RG_CONV, RG_WIDTH), lambda b, j: (0, 0)),
            vec(),
            pl.BlockSpec((RG_BLOCKS, RG_BW, 2 * RG_BW), lambda b, j: (0, 0, 0)),
            vec(), vec(), vec(),
        ],
        out_specs=pl.BlockSpec((rows, RG_WIDTH), lambda b, j: (row_map(b, j), 0)),
        out_shape=jax.ShapeDtypeStruct((t, RG_WIDTH), F32),
        scratch_shapes=[
            pltpu.VMEM((rows + 8, RG_WIDTH), F32),
            pltpu.VMEM((1, RG_WIDTH), F32),
        ],
        compiler_params=_cparams(("parallel", "arbitrary")),
        name="rglru_core",
    )(proj, proj, conv_w, conv_b.reshape(1, RG_WIDTH), wgate_bf, b_rg.reshape(1, RG_WIDTH),
      b_ig.reshape(1, RG_WIDTH), lam.reshape(1, RG_WIDTH))


def _fox_bias_selectors():
    piece = lambda part, head: part * FOX_HEADS + head
    one = 3 * FOX_HEADS
    sel_q = np.zeros((LANES, FOX_PAIRS * LANES), np.float32)
    sel_ka = np.zeros((LANES, FOX_PAIRS * LANES), np.float32)
    sel_kb = np.zeros((LANES, FOX_PAIRS * LANES), np.float32)
    for p in range(FOX_PAIRS):
        base = p * LANES
        for side, head in enumerate((2 * p, 2 * p + 1)):
            sel_k = sel_ka if side == 0 else sel_kb
            for part in range(3):
                sel_q[piece(part, head), base + 6 * side + part] = 1.0
                sel_q[one, base + 6 * side + 3 + part] = 1.0
                sel_k[one, base + 6 * side + part] = 1.0
                sel_k[piece(part, head), base + 6 * side + 3 + part] = -1.0
    return sel_q, sel_ka, sel_kb


def _fox_prep_kernel(q_ref, k_ref, v_ref, f_ref, bf_ref, gq_ref, gk_ref, grp_ref, tri_ref,
                     sq_ref, ska_ref, skb_ref, qa_ref, ka_ref, kb_ref, vb_ref, dcar_ref):
    @pl.when(pl.program_id(1) == 0)
    def _():
        dcar_ref[...] = jnp.zeros_like(dcar_ref)

    lane = lax.broadcasted_iota(I32, (1, LANES), 1)
    log_f = -_softplus(-(f_ref[...] + bf_ref[...]))
    log_f = jnp.where(lane < FOX_HEADS, log_f, 0.0)
    dcum = jnp.dot(tri_ref[...], log_f, preferred_element_type=F32,
                   precision=lax.Precision.HIGHEST) + dcar_ref[...]
    dcar_ref[...] = dcum[dcum.shape[0] - 1:, :]

    d2 = dcum * LOG2E
    d_hi = d2.astype(BF16)
    rem = d2 - d_hi.astype(F32)
    d_mid = rem.astype(BF16)
    d_lo = (rem - d_mid.astype(F32)).astype(BF16)
    hi, mid, lo = (pltpu.roll(p.astype(F32), s, axis=1) for p, s in
                   ((d_hi, 0), (d_mid, FOX_HEADS), (d_lo, 2 * FOX_HEADS)))
    pieces = jnp.where(lane < FOX_HEADS, hi,
                       jnp.where(lane < 2 * FOX_HEADS, mid,
                                 jnp.where(lane < 3 * FOX_HEADS, lo,
                                           jnp.where(lane == 3 * FOX_HEADS, 1.0, 0.0)))).astype(BF16)
    bias_q = jnp.dot(pieces, sq_ref[...], preferred_element_type=F32).astype(BF16)
    bias_ka = jnp.dot(pieces, ska_ref[...], preferred_element_type=F32).astype(BF16)
    bias_kb = jnp.dot(pieces, skb_ref[...], preferred_element_type=F32).astype(BF16)

    grp = grp_ref[...]
    first_head = lane < FOX_HD

    def rms(xt, gain):
        sq = xt * xt
        sq_hi = sq.astype(BF16)
        sq_lo = (sq - sq_hi.astype(F32)).astype(BF16)
        ssum = (jnp.dot(sq_hi, grp, preferred_element_type=F32)
                + jnp.dot(sq_lo, grp, preferred_element_type=F32))
        return xt * lax.rsqrt(ssum * (1.0 / FOX_HD) + LN_EPS) * gain

    for p in range(FOX_PAIRS):
        src = slice(p * LANES, (p + 1) * LANES)
        dst = slice(p * FOX_AUG, p * FOX_AUG + LANES)
        dst_bias = slice(p * FOX_AUG + LANES, (p + 1) * FOX_AUG)
        qn = rms(q_ref[:, src], gq_ref[...] * (FOX_HD ** -0.5 * LOG2E)).astype(BF16)
        kn = rms(k_ref[:, src], gk_ref[...]).astype(BF16)
        qa_ref[:, dst] = qn
        qa_ref[:, dst_bias] = bias_q[:, src]
        ka_ref[:, dst] = jnp.where(first_head, kn, jnp.zeros_like(kn))
        ka_ref[:, dst_bias] = bias_ka[:, src]
        kb_ref[:, dst] = jnp.where(first_head, jnp.zeros_like(kn), kn)
        kb_ref[:, dst_bias] = bias_kb[:, src]
    vb_ref[...] = v_ref[...].astype(BF16)


def _fox_prep_call(proj, b_f, q_norm_g, k_norm_g, batch, seq):
    t = proj.shape[0]
    d = FOX_HEADS * FOX_HD
    rows = min(FOX_PREP_ROWS, seq)
    per_seq = seq // rows
    row_map = lambda b, j: b * per_seq + j
    sel_q, sel_ka, sel_kb = (jnp.asarray(s, BF16) for s in _fox_bias_selectors())
    grp = np.kron(np.eye(2, dtype=np.float32), np.ones((FOX_HD, FOX_HD), np.float32))
    tri = np.tril(np.ones((rows, rows), np.float32))
    bf_pad = jnp.zeros((1, LANES), F32).at[0, :FOX_HEADS].set(b_f)
    gq2 = jnp.tile(q_norm_g, 2).reshape(1, LANES)
    gk2 = jnp.tile(k_norm_g, 2).reshape(1, LANES)
    const = lambda shape: pl.BlockSpec(shape, lambda b, j: tuple(0 for _ in shape))
    aug = jax.ShapeDtypeStruct((t, FOX_PAIRS * FOX_AUG), BF16)
    aug_spec = pl.BlockSpec((rows, FOX_PAIRS * FOX_AUG), lambda b, j: (row_map(b, j), 0))
    return pl.pallas_call(
        _fox_prep_kernel,
        grid=(batch, per_seq),
        in_specs=[
            pl.BlockSpec((rows, d), lambda b, j: (row_map(b, j), 0)),
            pl.BlockSpec((rows, d), lambda b, j: (row_map(b, j), 1)),
            pl.BlockSpec((rows, d), lambda b, j: (row_map(b, j), 2)),
            pl.BlockSpec((rows, LANES), lambda b, j: (row_map(b, j), 4 * d // LANES)),
            const((1, LANES)), const((1, LANES)), const((1, LANES)),
            const((LANES, LANES)), const((rows, rows)),
            const((LANES, FOX_PAIRS * LANES)), const((LANES, FOX_PAIRS * LANES)),
            const((LANES, FOX_PAIRS * LANES)),
        ],
        out_specs=[aug_spec, aug_spec, aug_spec,
                   pl.BlockSpec((rows, d), lambda b, j: (row_map(b, j), 0))],
        out_shape=[aug, aug, aug, jax.ShapeDtypeStruct((t, d), BF16)],
        scratch_shapes=[pltpu.VMEM((1, LANES), F32)],
        compiler_params=_cparams(("parallel", "arbitrary")),
        name="fox_prep",
    )(proj, proj, proj, proj, bf_pad, gq2, gk2, jnp.asarray(grp, BF16), jnp.asarray(tri),
      sel_q, sel_ka, sel_kb)


def _fox_attn_kernel(qa_ref, ka_ref, kb_ref, v_ref, og_ref, o_ref, m_ref, l_ref, acc_ref):
    qi = pl.program_id(2)
    tq = qa_ref.shape[0]
    sub = lax.broadcasted_iota(I32, (LANES, 1), 0)
    first_head = sub < FOX_HD

    m_ref[...] = jnp.full_like(m_ref, NEG_BIG)
    l_ref[...] = jnp.zeros_like(l_ref)
    acc_ref[...] = jnp.zeros_like(acc_ref)

    def steps(blocks):
        q = qa_ref[...]
        scored = []
        for ki, masked in blocks:
            rows = pl.ds(pl.multiple_of(ki * tq, tq), tq)
            scored.append((rows, masked, [
                lax.dot_general(k_ref[rows, :], q, (((1,), (1,)), ((), ())), preferred_element_type=F32)
                for k_ref in (ka_ref, kb_ref)]))
        for rows, masked, sts in scored:
            v = v_ref[rows, :]
            pv, alpha = [], []
            for side, st in enumerate(sts):
                if masked:
                    key = lax.broadcasted_iota(I32, st.shape, 0)
                    qry = lax.broadcasted_iota(I32, st.shape, 1)
                    st = jnp.where(key <= qry, st, NEG_BIG)
                m_prev = m_ref[side:side + 1, :]
                m_new = jnp.maximum(m_prev, jnp.max(st, axis=0, keepdims=True))
                a = jnp.exp2(m_prev - m_new)
                p = jnp.exp2(st - m_new)
                l_ref[side:side + 1, :] = a * l_ref[side:side + 1, :] + jnp.sum(p, axis=0, keepdims=True)
                m_ref[side:side + 1, :] = m_new
                pv.append(lax.dot_general(v, p.astype(BF16), (((0,), (0,)), ((), ())),
                                          preferred_element_type=F32))
                alpha.append(a)
            acc_ref[...] = (acc_ref[...] * jnp.where(first_head, alpha[0], alpha[1])
                            + jnp.where(first_head, pv[0], pv[1]))

    def body(j, carry):
        steps([(2 * j, False), (2 * j + 1, False)])
        return carry

    lax.fori_loop(0, qi // 2, body, 0)

    @pl.when(qi % 2 == 1)
    def _():
        steps([(qi - 1, False), (qi, True)])

    @pl.when(qi % 2 == 0)
    def _():
        steps([(qi, True)])

    out_t = acc_ref[...] / jnp.where(first_head, l_ref[0:1, :], l_ref[1:2, :])
    o_ref[...] = out_t.T * jax.nn.sigmoid(og_ref[...])


def _fox_attn_call(qa, ka, kb, vb, proj, batch, seq):
    t = qa.shape[0]
    d = FOX_HEADS * FOX_HD
    tq = min(FOX_TQ, seq)
    nq = seq // tq
    q_map = lambda b, p, i: (b * nq + i, p)
    kv_map = lambda b, p, i: (b, p)
    return pl.pallas_call(
        _fox_attn_kernel,
        grid=(batch, FOX_PAIRS, nq),
        in_specs=[
            pl.BlockSpec((tq, FOX_AUG), q_map),
            pl.BlockSpec((seq, FOX_AUG), kv_map),
            pl.BlockSpec((seq, FOX_AUG), kv_map),
            pl.BlockSpec((seq, LANES), kv_map),
            pl.BlockSpec((tq, LANES), lambda b, p, i: (b * nq + i, 3 * d // LANES + p)),
        ],
        out_specs=pl.BlockSpec((tq, LANES), q_map),
        out_shape=jax.ShapeDtypeStruct((t, d), F32),
        scratch_shapes=[
            pltpu.VMEM((2, tq), F32),
            pltpu.VMEM((2, tq), F32),
            pltpu.VMEM((LANES, tq), F32),
        ],
        compiler_params=_cparams(("parallel", "parallel", "arbitrary")),
        name="fox_attn",
    )(qa, ka, kb, vb, proj)


def store_token_tiles(ref, val):
    n, width = val.shape
    c = width // LANES
    for j in range(c):
        ref[pl.ds(j, n, stride=c), :] = val[:, j * LANES:(j + 1) * LANES]


def load_token_tiles(ref, n):
    c = ref.shape[0] // n
    return jnp.concatenate([ref[pl.ds(j, n, stride=c), :] for j in range(c)], axis=-1)


def _router_kernel(x_ref, sc_ref, sh_ref, whi_ref, wlo_ref, b_ref, h_ref, idx_ref, wt_ref, cnt_ref):
    h = x_ref[...] * (1.0 + sc_ref[0]) + sh_ref[0]
    store_token_tiles(h_ref, h)
    h_hi = h.astype(BF16)
    h_lo = (h - h_hi.astype(F32)).astype(BF16)
    logits = (jnp.dot(h_hi, whi_ref[...], preferred_element_type=F32)
              + jnp.dot(h_lo, whi_ref[...], preferred_element_type=F32)
              + jnp.dot(h_hi, wlo_ref[...], preferred_element_type=F32)
              + b_ref[...])
    lane = lax.broadcasted_iota(I32, logits.shape, 1)
    logits = jnp.where(lane < N_EXPERTS, logits, NEG_BIG)
    idx_out = jnp.zeros(logits.shape, I32)
    wt_out = jnp.zeros(logits.shape, F32)
    picked = jnp.zeros(logits.shape, I32)
    top0 = None
    denom = None
    for k in range(TOP_K):
        m = jnp.max(logits, axis=-1, keepdims=True)
        sel = jnp.min(jnp.where(logits == m, lane, LANES), axis=-1, keepdims=True)
        hit = lane == sel
        picked = picked + hit.astype(I32)
        logits = jnp.where(hit, NEG_BIG, logits)
        if k == 0:
            top0 = m
        e = jnp.exp(m - top0)
        denom = e if k == 0 else denom + e
        idx_out = jnp.where(lane == k, sel, idx_out)
        wt_out = jnp.where(lane == k, e, wt_out)
    idx_ref[...] = idx_out
    wt_ref[...] = wt_out / denom
    cnt_ref[0] = jnp.sum(picked, axis=0, keepdims=True)


def _router_call(x, sc, sh, w_hi, w_lo, b_pad, seq, tm):
    t, d = x.shape
    per_seq = seq // tm
    const = lambda shape: pl.BlockSpec(shape, lambda i: tuple(0 for _ in shape))
    return pl.pallas_call(
        _router_kernel,
        grid=(t // tm,),
        in_specs=[
            pl.BlockSpec((tm, d), lambda i: (i, 0)),
            pl.BlockSpec((1, 1, d), lambda i: (i // per_seq, 0, 0)),
            pl.BlockSpec((1, 1, d), lambda i: (i // per_seq, 0, 0)),
            const((d, LANES)), const((d, LANES)), const((1, LANES)),
        ],
        out_specs=[
            pl.BlockSpec((tm * (d // LANES), LANES), lambda i: (i, 0)),
            pl.BlockSpec((tm, LANES), lambda i: (i, 0)),
            pl.BlockSpec((tm, LANES), lambda i: (i, 0)),
            pl.BlockSpec((1, 1, LANES), lambda i: (i, 0, 0)),
        ],
        out_shape=[
            jax.ShapeDtypeStruct((t * (d // LANES), LANES), F32),
            jax.ShapeDtypeStruct((t, LANES), I32),
            jax.ShapeDtypeStruct((t, LANES), F32),
            jax.ShapeDtypeStruct((t // tm, 1, LANES), I32),
        ],
        compiler_params=_cparams(("parallel",)),
        name="moe_router",
    )(x, sc, sh, w_hi, w_lo, b_pad)


MOE_GROUP = 2 * LANES


def _moe_kernel(fb_ref, nb_ref, src_ref, dst_ref, nact_ref,
                h_hbm, wgu_ref, bgu_ref, wdn_ref, bdn_ref, perm_ref,
                y_hbm,
                xa, xb, xc, ya, yb, yc, wgu_bf, wdn_bf, gsem, ssem):
    e = pl.program_id(0)
    n_act = nact_ref[0]
    rows = MOE_ROWS
    n_groups = wgu_ref.shape[3] // MOE_GROUP
    tile = xa.shape[0] // rows
    xbufs, ybufs = (xa, xb, xc), (ya, yb, yc)
    n_buf = len(xbufs)

    def gather_start(blk, r, buf):
        tok = src_ref[blk * rows + r]
        pltpu.make_async_copy(h_hbm.at[tok], xbufs[buf].at[pl.ds(r * tile, tile), :], gsem.at[buf]).start()

    def scatter_start(blk_ext, r, buf):
        row = dst_ref[blk_ext * rows + r]
        pltpu.make_async_copy(ybufs[buf].at[pl.ds(r * tile, tile), :], y_hbm.at[row], ssem.at[buf]).start(priority=1)

    def wait_gather(buf):
        pltpu.make_async_copy(xbufs[buf], xbufs[buf], gsem.at[buf]).wait()

    def wait_scatter(buf):
        pltpu.make_async_copy(ybufs[buf], ybufs[buf], ssem.at[buf]).wait()

    def rolled(start_fn, blk, buf):
        def body(r, c):
            start_fn(blk, r, buf)
            return c
        lax.fori_loop(0, rows, body, 0)

    @pl.when(e == 0)
    def _():
        rolled(gather_start, 0, 0)
        rolled(gather_start, 1, 1)
        for buf in range(n_buf):
            ybufs[buf][...] = jnp.zeros_like(ybufs[buf])
        for par in range(2):
            def zero_row(r, c, par=par):
                row = dst_ref[r] + (par - 1) * rows
                pltpu.make_async_copy(ybufs[par].at[pl.ds(r * tile, tile), :], y_hbm.at[row], ssem.at[par]).start()
                return c
            lax.fori_loop(0, rows, zero_row, 0)
            wait_scatter(par)

    def block_step(cur, i):
        prev = (cur + 2) % n_buf
        prev2 = (cur + 1) % n_buf
        wait_gather(cur)
        x = load_token_tiles(xbufs[cur], rows).astype(BF16)
        gu = jnp.dot(x, wgu_bf[...], preferred_element_type=F32) + bgu_ref[0, 0]
        parts = []
        for g in range(n_groups):
            gate = jnp.minimum(gu[:, g * MOE_GROUP:g * MOE_GROUP + LANES], SWIGLU_LIMIT)
            up = jnp.clip(gu[:, g * MOE_GROUP + LANES:(g + 1) * MOE_GROUP], -SWIGLU_LIMIT, SWIGLU_LIMIT)
            glu = gate * jax.nn.sigmoid(SWIGLU_ALPHA * gate)
            parts.append(((up + 1.0) * glu).astype(BF16))
        hmid = jnp.concatenate(parts, axis=-1)
        store_token_tiles(ybufs[cur], jnp.dot(hmid, wdn_bf[...], preferred_element_type=F32) + bdn_ref[0, 0])
        for r in range(rows):
            gather_start(i + 2, r, prev)
        for r in range(rows):
            scatter_start(i, r, prev)

        @pl.when(i >= 1)
        def _():
            wait_scatter(prev2)

        @pl.when(i == n_act - 1)
        def _():
            rolled(scatter_start, i + 1, cur)
            wait_scatter(prev)
            wait_scatter(cur)
            wait_gather(prev2)
            wait_gather(prev)

    perm = perm_ref[...]
    for g in range(n_groups):
        cs = slice(g * MOE_GROUP, (g + 1) * MOE_GROUP)
        wgu_bf[:, cs] = jnp.dot(wgu_ref[0, 0, :, cs].astype(BF16), perm,
                                preferred_element_type=F32).astype(BF16)
    wdn_bf[...] = wdn_ref[0, 0].astype(BF16)

    def expert_block(j, carry):
        i = fb_ref[e] + j
        for buf in range(n_buf):
            @pl.when(i % n_buf == buf)
            def _(buf=buf):
                block_step(buf, i)
        return carry

    lax.fori_loop(0, nb_ref[e], expert_block, 0)


def _moe_call(h, first_blk, n_blk, src, dst, nact, w_gu, b_gu, w_down, b_down, layer, n_out_rows):
    _, n_exp, d, f2 = w_gu.shape
    f = w_down.shape[2]
    perm = np.zeros((MOE_GROUP, MOE_GROUP), np.float32)
    for jj in range(LANES):
        perm[2 * jj, jj] = 1.0
        perm[2 * jj + 1, LANES + jj] = 1.0
    b_gu_grouped = b_gu[layer].reshape(n_exp, f2 // MOE_GROUP, LANES, 2).transpose(0, 1, 3, 2).reshape(
        1, n_exp, 1, f2)
    by_expert = lambda e, *_: (layer, e, 0, 0)
    grid_spec = pltpu.PrefetchScalarGridSpec(
        num_scalar_prefetch=5,
        grid=(n_exp,),
        in_specs=[
            pl.BlockSpec(memory_space=pl.ANY),
            pl.BlockSpec((1, 1, d, f2), by_expert),
            pl.BlockSpec((1, 1, 1, f2), lambda e, *_: (0, e, 0, 0)),
            pl.BlockSpec((1, 1, f, d), by_expert),
            pl.BlockSpec((1, 1, 1, d), by_expert),
            pl.BlockSpec((MOE_GROUP, MOE_GROUP), lambda e, *_: (0, 0)),
        ],
        out_specs=pl.BlockSpec(memory_space=pl.ANY),
        scratch_shapes=[
            *[pltpu.VMEM((MOE_ROWS * (d // LANES), LANES), F32) for _ in range(6)],
            pltpu.VMEM((d, f2), BF16),
            pltpu.VMEM((f, d), BF16),
            pltpu.SemaphoreType.DMA((3,)),
            pltpu.SemaphoreType.DMA((3,)),
        ],
    )
    return pl.pallas_call(
        _moe_kernel,
        grid_spec=grid_spec,
        out_shape=jax.ShapeDtypeStruct((n_out_rows, d // LANES, LANES), F32),
        compiler_params=_cparams(("arbitrary",)),
        name="moe_experts",
    )(first_blk, n_blk, src, dst, nact, h, w_gu, b_gu_grouped, w_down, b_down.reshape(b_down.shape[0], n_exp, 1, d),
      jnp.asarray(perm, BF16))


def _combine_kernel(y0_ref, y1_ref, y2_ref, y3_ref, wt_ref, x_ref, g_ref, lg_ref, lb_ref, o_ref):
    wt = wt_ref[...]
    n = x_ref.shape[0]
    y = load_token_tiles(y0_ref, n) * wt[:, 0:1]
    for k, y_ref in enumerate((y1_ref, y2_ref, y3_ref), start=1):
        y = y + load_token_tiles(y_ref, n) * wt[:, k:k + 1]
    z = ALPHA * x_ref[...] + (1.0 + g_ref[0]) * y
    o_ref[...] = _layer_norm_rows(z, lg_ref[...], lb_ref[...])


def _combine_call(y4, wt, x, gate, ln_g, ln_b, seq, tm):
    t, d = x.shape
    per_seq = seq // tm
    nblk = t // tm
    y_spec = lambda k: pl.BlockSpec((tm * (d // LANES), LANES), lambda i, k=k: (k * nblk + i, 0))
    return pl.pallas_call(
        _combine_kernel,
        grid=(nblk,),
        in_specs=[
            y_spec(0), y_spec(1), y_spec(2), y_spec(3),
            pl.BlockSpec((tm, LANES), lambda i: (i, 0)),
            pl.BlockSpec((tm, d), lambda i: (i, 0)),
            pl.BlockSpec((1, 1, d), lambda i: (i // per_seq, 0, 0)),
            pl.BlockSpec((1, d), lambda i: (0, 0)),
            pl.BlockSpec((1, d), lambda i: (0, 0)),
        ],
        out_specs=pl.BlockSpec((tm, d), lambda i: (i, 0)),
        out_shape=jax.ShapeDtypeStruct((t, d), F32),
        compiler_params=_cparams(("parallel",)),
        name="moe_combine_ln",
    )(y4, y4, y4, y4, wt, x, gate, ln_g.reshape(1, d), ln_b.reshape(1, d))


def _moe_plan(top_idx, block_counts, t):
    n_assign = t * TOP_K
    n_blocks = -(-(n_assign + N_EXPERTS * (MOE_ROWS - 1)) // MOE_ROWS)
    experts = jnp.arange(N_EXPERTS, dtype=I32)
    order = jnp.argsort(top_idx.reshape(-1)).astype(I32)
    counts = jnp.sum(block_counts[:, 0, :N_EXPERTS], axis=0)
    starts = jnp.cumsum(counts) - counts
    padded = (counts + MOE_ROWS - 1) // MOE_ROWS * MOE_ROWS
    pad_ends = jnp.cumsum(padded)
    pad_starts = pad_ends - padded
    blocks = jnp.arange(n_blocks + 2, dtype=I32)
    first_row = blocks * MOE_ROWS
    block_e = jnp.minimum(jnp.sum((pad_ends[None, :] <= first_row[:, None]).astype(I32), axis=1), N_EXPERTS - 1)
    of_block = lambda v: jnp.sum(jnp.where(block_e[:, None] == experts[None, :], v[None, :], 0), axis=1)
    seg_off = first_row - of_block(pad_starts)
    n_valid = jnp.where(first_row < pad_ends[-1], jnp.clip(of_block(counts) - seg_off, 0, MOE_ROWS), 0)
    r = jnp.arange(MOE_ROWS, dtype=I32)
    valid = r[None, :] < n_valid[:, None]
    assign = order[jnp.clip((of_block(starts) + seg_off)[:, None] + r[None, :], 0, n_assign - 1)]
    dump = n_assign + (blocks[:, None] % 2) * MOE_ROWS + r[None, :]
    src = jnp.where(valid, assign // TOP_K, 0).reshape(-1)
    dst = jnp.where(valid, (assign % TOP_K) * t + assign // TOP_K, dump)
    dst_ext = jnp.concatenate([dump[1], dst[:n_blocks].reshape(-1)])
    n_act = (pad_ends[-1] // MOE_ROWS).astype(I32).reshape(1)
    return pad_starts // MOE_ROWS, padded // MOE_ROWS, src, dst_ext, n_act, n_assign + 2 * MOE_ROWS


def _moe_layer(x, sc, sh, gate, ln_g, ln_b, w_router, b_router, w_gu, b_gu, w_down, b_down, layer, seq, tm):
    t, d = x.shape
    w_pad = jnp.zeros((d, LANES), F32).at[:, :N_EXPERTS].set(w_router)
    w_hi = w_pad.astype(BF16)
    w_lo = (w_pad - w_hi.astype(F32)).astype(BF16)
    b_pad = jnp.zeros((1, LANES), F32).at[0, :N_EXPERTS].set(b_router)
    h, idx, wt, block_counts = _router_call(x, sc, sh, w_hi, w_lo, b_pad, seq, tm)
    first_blk, n_blk, src, dst, n_act, n_rows = _moe_plan(idx[:, :TOP_K], block_counts, t)
    y4 = _moe_call(h.reshape(t, d // LANES, LANES), first_blk, n_blk, src, dst, n_act, w_gu, b_gu, w_down, b_down,
                   layer, n_rows)
    return _combine_call(y4.reshape(-1, LANES), wt, x, gate, ln_g, ln_b, seq, tm)


def _pad_cols(w, n):
    return jnp.pad(w, ((0, 0), (0, n - w.shape[1])))


def kernel(x, c, ada_w, ada_b, ln_g, ln_b, gla_w_in, gla_w_gate_up, gla_b_gate, gla_norm_g, gla_w_out, rg_w_in, rg_conv_w, rg_conv_b, rg_w_rg, rg_b_rg, rg_w_ig, rg_b_ig, rg_lambda, rg_w_out, fox_w_in, fox_b_f, fox_q_norm_g, fox_k_norm_g, fox_w_out, moe_w_router, moe_b_router, moe_w_gu, moe_b_gu, moe_w_down, moe_b_down):
    batch, seq, d = x.shape
    t = batch * seq
    tm = min(512, seq)
    depth = ada_w.shape[0]
    xt = x.reshape(t, d)
    mod = _ada_call(c, ada_w, ada_b)

    for l in range(depth):
        part = lambda i: mod[l, :, i * d:(i + 1) * d].reshape(batch, 1, d)
        sh1, sc1, g1, sh2, sc2, g2 = (part(i) for i in range(6))
        kind, j = l % 3, l // 3
        if kind == 0:
            n_in = 2 * GLA_DK + 2 * GLA_DV + LANES
            proj = _modmm_call(xt, sc1, sh1, _pad_cols(gla_w_in[j], n_in).astype(BF16), seq, tm)
            wg = jnp.pad(gla_w_gate_up[j], ((0, LANES - GLA_RANK), (0, 0))).astype(BF16)
            y = _gla_call(proj, wg, gla_b_gate[j], gla_norm_g[j], batch, seq)
            w_out = gla_w_out[j]
        elif kind == 1:
            proj = _modmm_call(xt, sc1, sh1, rg_w_in[j].astype(BF16), seq, tm)
            wgate = jnp.concatenate([rg_w_rg[j], rg_w_ig[j]], axis=-1).astype(BF16)
            y = _rg_call(proj, rg_conv_w[j], rg_conv_b[j], wgate, rg_b_rg[j], rg_b_ig[j],
                         rg_lambda[j], batch, seq)
            w_out = rg_w_out[j]
        else:
            w = fox_w_in[j]
            w = jnp.concatenate([w[:, :3 * d], w[:, 3 * d + FOX_HEADS:], w[:, 3 * d:3 * d + FOX_HEADS]], axis=1)
            proj = _modmm_call(xt, sc1, sh1, _pad_cols(w, 4 * d + LANES).astype(BF16), seq, tm)
            qa, ka, kb, vb = _fox_prep_call(proj, fox_b_f[j], fox_q_norm_g[j], fox_k_norm_g[j], batch, seq)
            y = _fox_attn_call(qa, ka, kb, vb, proj, batch, seq)
            w_out = fox_w_out[j]
        xt = _outln_call(y, w_out.astype(BF16), xt, g1, ln_g[l, 0], ln_b[l, 0], seq, tm)
        xt = _moe_layer(xt, sc2, sh2, g2, ln_g[l, 1], ln_b[l, 1], moe_w_router[l], moe_b_router[l],
                        moe_w_gu, moe_b_gu, moe_w_down, moe_b_down, l, seq, tm)
    return xt.reshape(batch, seq, d)
```

```python
import functools

import numpy as np
import jax
import jax.numpy as jnp
from jax import lax
from jax.experimental import pallas as pl
from jax.experimental.pallas import tpu as pltpu

F32 = jnp.float32
BF16 = jnp.bfloat16
I32 = jnp.int32

DEPTH = 4
ALPHA = (2.0 * DEPTH) ** 0.25
LN_EPS = 1e-5

GLA_HEADS = 4
GLA_HK = 128
GLA_HV = 256
GLA_DK = GLA_HEADS * GLA_HK
GLA_DV = GLA_HEADS * GLA_HV
GLA_RANK = 16
GLA_TAU = 16.0
GLA_CHUNK = 64
GLA_ROWS = 512

RG_BW = 128
RG_BLOCKS = 10
RG_WIDTH = RG_BW * RG_BLOCKS
RG_CONV = 4
RG_C = 8.0
RG_ROWS = 256

FOX_HD = 64
FOX_HEADS = 16
FOX_PAIRS = FOX_HEADS // 2
FOX_TQ = 512
FOX_PREP_ROWS = 256
FOX_AUG = 256

N_EXPERTS = 32
TOP_K = 4
SWIGLU_LIMIT = 7.0
SWIGLU_ALPHA = 1.702
MOE_ROWS = 256
NEG_BIG = -1e30
LOG2E = 1.4426950408889634

LANES = 128
VMEM_LIMIT = 56 * 1024 * 1024


def _cparams(sem, vmem=VMEM_LIMIT):
    return pltpu.CompilerParams(dimension_semantics=sem, vmem_limit_bytes=vmem)


def _softplus(z):
    return jnp.maximum(z, 0.0) + jnp.log(1.0 + jnp.exp(-jnp.abs(z)))


def _layer_norm_rows(z, g, b):
    mu = jnp.mean(z, axis=-1, keepdims=True)
    zc = z - mu
    var = jnp.mean(zc * zc, axis=-1, keepdims=True)
    return zc * lax.rsqrt(var + LN_EPS) * g + b


def _ada_kernel(c_ref, w_ref, b_ref, o_ref):
    c = c_ref[...]
    ca = c * jax.nn.sigmoid(c)
    o_ref[0] = jnp.dot(ca.astype(BF16), w_ref[0].astype(BF16), preferred_element_type=F32) + b_ref[0]


def _ada_call(c, ada_w, ada_b):
    depth, d, n = ada_w.shape
    b = c.shape[0]
    tn = n // 4
    return pl.pallas_call(
        _ada_kernel,
        grid=(depth, n // tn),
        in_specs=[
            pl.BlockSpec((b, d), lambda l, j: (0, 0)),
            pl.BlockSpec((1, d, tn), lambda l, j: (l, 0, j)),
            pl.BlockSpec((1, 1, tn), lambda l, j: (l, 0, j)),
        ],
        out_specs=pl.BlockSpec((1, b, tn), lambda l, j: (l, 0, j)),
        out_shape=jax.ShapeDtypeStruct((depth, b, n), F32),
        compiler_params=_cparams(("parallel", "parallel")),
        name="ada_mod",
    )(c, ada_w, ada_b.reshape(depth, 1, n))


def _modmm_kernel(x_ref, sc_ref, sh_ref, w_ref, o_ref):
    h = x_ref[...] * (1.0 + sc_ref[0]) + sh_ref[0]
    o_ref[...] = jnp.dot(h.astype(BF16), w_ref[...], preferred_element_type=F32)


def _modmm_call(x, sc, sh, w_bf, seq, tm):
    t, d = x.shape
    n = w_bf.shape[1]
    per_seq = seq // tm
    return pl.pallas_call(
        _modmm_kernel,
        grid=(t // tm,),
        in_specs=[
            pl.BlockSpec((tm, d), lambda i: (i, 0)),
            pl.BlockSpec((1, 1, d), lambda i: (i // per_seq, 0, 0)),
            pl.BlockSpec((1, 1, d), lambda i: (i // per_seq, 0, 0)),
            pl.BlockSpec((d, n), lambda i: (0, 0)),
        ],
        out_specs=pl.BlockSpec((tm, n), lambda i: (i, 0)),
        out_shape=jax.ShapeDtypeStruct((t, n), F32),
        compiler_params=_cparams(("parallel",)),
        name="mod_inproj",
    )(x, sc, sh, w_bf)


def _outln_kernel(y_ref, w_ref, x_ref, g_ref, lg_ref, lb_ref, o_ref):
    y = jnp.dot(y_ref[...].astype(BF16), w_ref[...], preferred_element_type=F32)
    z = ALPHA * x_ref[...] + (1.0 + g_ref[0]) * y
    o_ref[...] = _layer_norm_rows(z, lg_ref[...], lb_ref[...])


def _outln_call(y, w_bf, x, gate, ln_g, ln_b, seq, tm):
    t, k = y.shape
    d = x.shape[1]
    per_seq = seq // tm
    return pl.pallas_call(
        _outln_kernel,
        grid=(t // tm,),
        in_specs=[
            pl.BlockSpec((tm, k), lambda i: (i, 0)),
            pl.BlockSpec((k, d), lambda i: (0, 0)),
            pl.BlockSpec((tm, d), lambda i: (i, 0)),
            pl.BlockSpec((1, 1, d), lambda i: (i // per_seq, 0, 0)),
            pl.BlockSpec((1, d), lambda i: (0, 0)),
            pl.BlockSpec((1, d), lambda i: (0, 0)),
        ],
        out_specs=pl.BlockSpec((tm, d), lambda i: (i, 0)),
        out_shape=jax.ShapeDtypeStruct((t, d), F32),
        compiler_params=_cparams(("parallel",)),
        name="outproj_ln",
    )(y, w_bf, x, gate, ln_g.reshape(1, d), ln_b.reshape(1, d))


def _gla_kernel(q_ref, k_ref, v_ref, r_ref, g_ref, wg_ref, bg_ref, ng_ref, tri_ref,
                o_ref, la_ref, st_ref):
    c_rows = GLA_CHUNK
    half = c_rows // 2

    @pl.when(pl.program_id(1) == 0)
    def _():
        st_ref[...] = jnp.zeros_like(st_ref)

    z = jnp.dot(g_ref[...].astype(BF16), wg_ref[...], preferred_element_type=F32) + bg_ref[...]
    la_ref[...] = -_softplus(-z) * (1.0 / GLA_TAU)

    row = lax.broadcasted_iota(I32, (c_rows, c_rows), 0)
    col = lax.broadcasted_iota(I32, (c_rows, c_rows), 1)
    causal = col <= row
    tri = tri_ref[...]
    n_chunks = q_ref.shape[0] // c_rows

    def chunk(c):
        rows = pl.ds(c * c_rows, c_rows)
        bc = jnp.dot(tri, la_ref[rows, :], preferred_element_type=F32, precision=lax.Precision.HIGHEST)
        b_mid = bc[half - 1:half, :]
        b_last = bc[c_rows - 1:c_rows, :]
        e_q_mid = jnp.exp(bc - b_mid)
        e_k_mid = jnp.exp(b_mid - bc)
        e_q = jnp.exp(bc)
        e_k_end = jnp.exp(b_last - bc)
        dec = jnp.exp(b_last)
        qc = q_ref[rows, :] * (GLA_HK ** -0.5)
        kc = k_ref[rows, :]
        for h in range(GLA_HEADS):
            ks = slice(h * GLA_HK, (h + 1) * GLA_HK)
            vs = slice(h * GLA_HV, (h + 1) * GLA_HV)
            qm = (qc[:, ks] * e_q_mid[:, ks]).astype(BF16)
            km = (kc[:, ks] * e_k_mid[:, ks]).astype(BF16)
            s = lax.dot_general(qm, km, (((1,), (1,)), ((), ())), preferred_element_type=F32)
            s = jnp.where(causal, s, 0.0)
            vh = v_ref[rows, vs].astype(BF16)
            o = jnp.dot(s.astype(BF16), vh, preferred_element_type=F32)
            qd = (qc[:, ks] * e_q[:, ks]).astype(BF16)
            st = st_ref[h]
            o = o + lax.dot_general(qd, st.astype(BF16), (((1,), (1,)), ((), ())),
                                    preferred_element_type=F32)
            ke = (kc[:, ks] * e_k_end[:, ks]).astype(BF16)
            st_ref[h] = st * dec[:, ks] + lax.dot_general(
                vh, ke, (((0,), (0,)), ((), ())), preferred_element_type=F32)
            ms = jnp.mean(o * o, axis=-1, keepdims=True)
            y = o * lax.rsqrt(ms + LN_EPS) * ng_ref[:, vs]
            rr = r_ref[rows, vs]
            o_ref[rows, vs] = (y * (rr * jax.nn.sigmoid(rr))).astype(o_ref.dtype)

    for c in range(n_chunks):
        chunk(c)


def _gla_call(proj, wg_pad_bf, b_gate, norm_g, batch, seq):
    t = proj.shape[0]
    rows = min(GLA_ROWS, seq)
    per_seq = seq // rows
    tri = jnp.asarray(np.tril(np.ones((GLA_CHUNK, GLA_CHUNK), np.float32)))
    row_map = lambda b, j: b * per_seq + j
    return pl.pallas_call(
        _gla_kernel,
        grid=(batch, per_seq),
        in_specs=[
            pl.BlockSpec((rows, GLA_DK), lambda b, j: (row_map(b, j), 0)),
            pl.BlockSpec((rows, GLA_DK), lambda b, j: (row_map(b, j), 1)),
            pl.BlockSpec((rows, GLA_DV), lambda b, j: (row_map(b, j), 1)),
            pl.BlockSpec((rows, GLA_DV), lambda b, j: (row_map(b, j), 2)),
            pl.BlockSpec((rows, LANES), lambda b, j: (row_map(b, j), (2 * GLA_DK + 2 * GLA_DV) // LANES)),
            pl.BlockSpec((LANES, GLA_DK), lambda b, j: (0, 0)),
            pl.BlockSpec((1, GLA_DK), lambda b, j: (0, 0)),
            pl.BlockSpec((1, GLA_DV), lambda b, j: (0, 0)),
            pl.BlockSpec((GLA_CHUNK, GLA_CHUNK), lambda b, j: (0, 0)),
        ],
        out_specs=pl.BlockSpec((rows, GLA_DV), lambda b, j: (row_map(b, j), 0)),
        out_shape=jax.ShapeDtypeStruct((t, GLA_DV), BF16),
        scratch_shapes=[
            pltpu.VMEM((rows, GLA_DK), F32),
            pltpu.VMEM((GLA_HEADS, GLA_HV, GLA_HK), F32),
        ],
        compiler_params=_cparams(("parallel", "arbitrary")),
        name="gla_core",
    )(proj, proj, proj, proj, proj, wg_pad_bf, b_gate.reshape(1, GLA_DK), norm_g.reshape(1, GLA_DV), tri)


def _rg_kernel(y_ref, x_ref, cw_ref, cb_ref, wg_ref, brg_ref, big_ref, lam_ref,
               o_ref, xext_ref, hcar_ref):
    rows = x_ref.shape[0]

    @pl.when(pl.program_id(1) == 0)
    def _():
        xext_ref[0:8, :] = jnp.zeros((8, RG_WIDTH), F32)
        hcar_ref[...] = jnp.zeros_like(hcar_ref)

    xext_ref[8:8 + rows, :] = x_ref[...]
    xc = cb_ref[...] + xext_ref[pl.ds(8 - (RG_CONV - 1), rows), :] * cw_ref[0:1, :]
    for j in range(1, RG_CONV):
        xc = xc + xext_ref[pl.ds(8 - (RG_CONV - 1) + j, rows), :] * cw_ref[j:j + 1, :]
    xext_ref[0:8, :] = x_ref[rows - 8:rows, :]

    r_parts, i_parts = [], []
    for n in range(RG_BLOCKS):
        cs = slice(n * RG_BW, (n + 1) * RG_BW)
        g = jnp.dot(xc[:, cs].astype(BF16), wg_ref[n], preferred_element_type=F32)
        r_parts.append(g[:, :RG_BW])
        i_parts.append(g[:, RG_BW:])
    r = jax.nn.sigmoid(jnp.concatenate(r_parts, axis=-1) + brg_ref[...])
    ig = jax.nn.sigmoid(jnp.concatenate(i_parts, axis=-1) + big_ref[...])

    log_a = -RG_C * r * _softplus(-lam_ref[...])
    a = jnp.exp(log_a)
    th = jnp.tanh(log_a)
    u = jnp.sqrt(-2.0 * th / (1.0 - th)) * (ig * xc)

    n_grp = rows // 8
    a3 = a.reshape(n_grp, 8, RG_WIDTH)
    h3 = u.reshape(n_grp, 8, RG_WIDTH)
    sidx = lax.broadcasted_iota(I32, (1, 8, 1), 1)
    for d in (1, 2, 4):
        keep = sidx >= d
        sh_a = pltpu.roll(a3, d, axis=1)
        sh_h = pltpu.roll(h3, d, axis=1)
        h3 = jnp.where(keep, a3 * sh_h + h3, h3)
        a3 = jnp.where(keep, a3 * sh_a, a3)
    carry = hcar_ref[...]
    groups = []
    for g in range(n_grp):
        hg = h3[g] + a3[g] * carry
        carry = hg[7:8, :]
        groups.append(hg)
    h = jnp.concatenate(groups, axis=0)
    hcar_ref[...] = carry

    yb = y_ref[...]
    gelu = 0.5 * yb * (1.0 + jnp.tanh(0.7978845608028654 * (yb + 0.044715 * (yb * yb * yb))))
    o_ref[...] = (h * gelu).astype(o_ref.dtype)


def _rg_call(proj, conv_w, conv_b, wgate_bf, b_rg, b_ig, lam, batch, seq):
    t = proj.shape[0]
    rows = min(RG_ROWS, seq)
    per_seq = seq // rows
    row_map = lambda b, j: b * per_seq + j
    vec = lambda: pl.BlockSpec((1, RG_WIDTH), lambda b, j: (0, 0))
    return pl.pallas_call(
        _rg_kernel,
        grid=(batch, per_seq),
        in_specs=[
            pl.BlockSpec((rows, RG_WIDTH), lambda b, j: (row_map(b, j), 0)),
            pl.BlockSpec((rows, RG_WIDTH), lambda b, j: (row_map(b, j), 1)),
            pl.BlockSpec((RG_CONV, RG_WIDTH), lambda b, j: (0, 0)),
            vec(),
            pl.BlockSpec((RG_BLOCKS, RG_BW, 2 * RG_BW), lambda b, j: (0, 0, 0)),
            vec(), vec(), vec(),
        ],
        out_specs=pl.BlockSpec((rows, RG_WIDTH), lambda b, j: (row_map(b, j), 0)),
        out_shape=jax.ShapeDtypeStruct((t, RG_WIDTH), BF16),
        scratch_shapes=[
            pltpu.VMEM((rows + 8, RG_WIDTH), F32),
            pltpu.VMEM((1, RG_WIDTH), F32),
        ],
        compiler_params=_cparams(("parallel", "arbitrary")),
        name="rglru_core",
    )(proj, proj, conv_w, conv_b.reshape(1, RG_WIDTH), wgate_bf, b_rg.reshape(1, RG_WIDTH),
      b_ig.reshape(1, RG_WIDTH), lam.reshape(1, RG_WIDTH))


def _fox_bias_selectors():
    piece = lambda part, head: part * FOX_HEADS + head
    one = 3 * FOX_HEADS
    sel_q = np.zeros((LANES, FOX_PAIRS * LANES), np.float32)
    sel_ka = np.zeros((LANES, FOX_PAIRS * LANES), np.float32)
    sel_kb = np.zeros((LANES, FOX_PAIRS * LANES), np.float32)
    for p in range(FOX_PAIRS):
        base = p * LANES
        for side, head in enumerate((2 * p, 2 * p + 1)):
            sel_k = sel_ka if side == 0 else sel_kb
            for part in range(3):
                sel_q[piece(part, head), base + 6 * side + part] = 1.0
                sel_q[one, base + 6 * side + 3 + part] = 1.0
                sel_k[one, base + 6 * side + part] = 1.0
                sel_k[piece(part, head), base + 6 * side + 3 + part] = -1.0
    return sel_q, sel_ka, sel_kb


def _fox_prep_kernel(q_ref, k_ref, v_ref, f_ref, bf_ref, gq_ref, gk_ref, grp_ref, tri_ref,
                     sq_ref, ska_ref, skb_ref, qa_ref, ka_ref, kb_ref, vb_ref, dcar_ref):
    @pl.when(pl.program_id(1) == 0)
    def _():
        dcar_ref[...] = jnp.zeros_like(dcar_ref)

    lane = lax.broadcasted_iota(I32, (1, LANES), 1)
    log_f = -_softplus(-(f_ref[...] + bf_ref[...]))
    log_f = jnp.where(lane < FOX_HEADS, log_f, 0.0)
    dcum = jnp.dot(tri_ref[...], log_f, preferred_element_type=F32,
                   precision=lax.Precision.HIGHEST) + dcar_ref[...]
    dcar_ref[...] = dcum[dcum.shape[0] - 1:, :]

    d2 = dcum * LOG2E
    d_hi = d2.astype(BF16)
    rem = d2 - d_hi.astype(F32)
    d_mid = rem.astype(BF16)
    d_lo = (rem - d_mid.astype(F32)).astype(BF16)
    hi, mid, lo = (pltpu.roll(p.astype(F32), s, axis=1) for p, s in
                   ((d_hi, 0), (d_mid, FOX_HEADS), (d_lo, 2 * FOX_HEADS)))
    pieces = jnp.where(lane < FOX_HEADS, hi,
                       jnp.where(lane < 2 * FOX_HEADS, mid,
                                 jnp.where(lane < 3 * FOX_HEADS, lo,
                                           jnp.where(lane == 3 * FOX_HEADS, 1.0, 0.0)))).astype(BF16)
    bias_q = jnp.dot(pieces, sq_ref[...], preferred_element_type=F32).astype(BF16)
    bias_ka = jnp.dot(pieces, ska_ref[...], preferred_element_type=F32).astype(BF16)
    bias_kb = jnp.dot(pieces, skb_ref[...], preferred_element_type=F32).astype(BF16)

    grp = grp_ref[...]
    first_head = lane < FOX_HD

    def rms(xt, gain):
        sq = xt * xt
        sq_hi = sq.astype(BF16)
        sq_lo = (sq - sq_hi.astype(F32)).astype(BF16)
        ssum = (jnp.dot(sq_hi, grp, preferred_element_type=F32)
                + jnp.dot(sq_lo, grp, preferred_element_type=F32))
        return xt * lax.rsqrt(ssum * (1.0 / FOX_HD) + LN_EPS) * gain

    for p in range(FOX_PAIRS):
        src = slice(p * LANES, (p + 1) * LANES)
        dst = slice(p * FOX_AUG, p * FOX_AUG + LANES)
        dst_bias = slice(p * FOX_AUG + LANES, (p + 1) * FOX_AUG)
        qn = rms(q_ref[:, src], gq_ref[...] * (FOX_HD ** -0.5 * LOG2E)).astype(BF16)
        kn = rms(k_ref[:, src], gk_ref[...]).astype(BF16)
        qa_ref[:, dst] = qn
        qa_ref[:, dst_bias] = bias_q[:, src]
        ka_ref[:, dst] = jnp.where(first_head, kn, jnp.zeros_like(kn))
        ka_ref[:, dst_bias] = bias_ka[:, src]
        kb_ref[:, dst] = jnp.where(first_head, jnp.zeros_like(kn), kn)
        kb_ref[:, dst_bias] = bias_kb[:, src]
    vb_ref[...] = v_ref[...].astype(BF16)


def _fox_prep_call(proj, b_f, q_norm_g, k_norm_g, batch, seq):
    t = proj.shape[0]
    d = FOX_HEADS * FOX_HD
    rows = min(FOX_PREP_ROWS, seq)
    per_seq = seq // rows
    row_map = lambda b, j: b * per_seq + j
    sel_q, sel_ka, sel_kb = (jnp.asarray(s, BF16) for s in _fox_bias_selectors())
    grp = np.kron(np.eye(2, dtype=np.float32), np.ones((FOX_HD, FOX_HD), np.float32))
    tri = np.tril(np.ones((rows, rows), np.float32))
    bf_pad = jnp.zeros((1, LANES), F32).at[0, :FOX_HEADS].set(b_f)
    gq2 = jnp.tile(q_norm_g, 2).reshape(1, LANES)
    gk2 = jnp.tile(k_norm_g, 2).reshape(1, LANES)
    const = lambda shape: pl.BlockSpec(shape, lambda b, j: tuple(0 for _ in shape))
    aug = jax.ShapeDtypeStruct((t, FOX_PAIRS * FOX_AUG), BF16)
    aug_spec = pl.BlockSpec((rows, FOX_PAIRS * FOX_AUG), lambda b, j: (row_map(b, j), 0))
    return pl.pallas_call(
        _fox_prep_kernel,
        grid=(batch, per_seq),
        in_specs=[
            pl.BlockSpec((rows, d), lambda b, j: (row_map(b, j), 0)),
            pl.BlockSpec((rows, d), lambda b, j: (row_map(b, j), 1)),
            pl.BlockSpec((rows, d), lambda b, j: (row_map(b, j), 2)),
            pl.BlockSpec((rows, LANES), lambda b, j: (row_map(b, j), 4 * d // LANES)),
            const((1, LANES)), const((1, LANES)), const((1, LANES)),
            const((LANES, LANES)), const((rows, rows)),
            const((LANES, FOX_PAIRS * LANES)), const((LANES, FOX_PAIRS * LANES)),
            const((LANES, FOX_PAIRS * LANES)),
        ],
        out_specs=[aug_spec, aug_spec, aug_spec,
                   pl.BlockSpec((rows, d), lambda b, j: (row_map(b, j), 0))],
        out_shape=[aug, aug, aug, jax.ShapeDtypeStruct((t, d), BF16)],
        scratch_shapes=[pltpu.VMEM((1, LANES), F32)],
        compiler_params=_cparams(("parallel", "arbitrary")),
        name="fox_prep",
    )(proj, proj, proj, proj, bf_pad, gq2, gk2, jnp.asarray(grp, BF16), jnp.asarray(tri),
      sel_q, sel_ka, sel_kb)


def _fox_attn_kernel(qa_ref, ka_ref, kb_ref, v_ref, og_ref, o_ref, m_ref, l_ref, acc_ref):
    qi = pl.program_id(2)
    tq = qa_ref.shape[0]
    sub = lax.broadcasted_iota(I32, (LANES, 1), 0)
    first_head = sub < FOX_HD

    m_ref[...] = jnp.full_like(m_ref, NEG_BIG)
    l_ref[...] = jnp.zeros_like(l_ref)
    acc_ref[...] = jnp.zeros_like(acc_ref)

    def steps(blocks):
        q = qa_ref[...]
        scored = []
        for ki, masked in blocks:
            rows = pl.ds(pl.multiple_of(ki * tq, tq), tq)
            scored.append((rows, masked, [
                lax.dot_general(k_ref[rows, :], q, (((1,), (1,)), ((), ())), preferred_element_type=F32)
                for k_ref in (ka_ref, kb_ref)]))
        for rows, masked, sts in scored:
            v = v_ref[rows, :]
            pv, alpha = [], []
            for side, st in enumerate(sts):
                if masked:
                    key = lax.broadcasted_iota(I32, st.shape, 0)
                    qry = lax.broadcasted_iota(I32, st.shape, 1)
                    st = jnp.where(key <= qry, st, NEG_BIG)
                m_prev = m_ref[side:side + 1, :]
                m_new = jnp.maximum(m_prev, jnp.max(st, axis=0, keepdims=True))
                a = jnp.exp2(m_prev - m_new)
                p = jnp.exp2(st - m_new)
                l_ref[side:side + 1, :] = a * l_ref[side:side + 1, :] + jnp.sum(p, axis=0, keepdims=True)
                m_ref[side:side + 1, :] = m_new
                pv.append(lax.dot_general(v, p.astype(BF16), (((0,), (0,)), ((), ())),
                                          preferred_element_type=F32))
                alpha.append(a)
            acc_ref[...] = (acc_ref[...] * jnp.where(first_head, alpha[0], alpha[1])
                            + jnp.where(first_head, pv[0], pv[1]))

    def body(j, carry):
        steps([(2 * j, False), (2 * j + 1, False)])
        return carry

    lax.fori_loop(0, qi // 2, body, 0)

    @pl.when(qi % 2 == 1)
    def _():
        steps([(qi - 1, False), (qi, True)])

    @pl.when(qi % 2 == 0)
    def _():
        steps([(qi, True)])

    out_t = acc_ref[...] / jnp.where(first_head, l_ref[0:1, :], l_ref[1:2, :])
    o_ref[...] = (out_t.T * jax.nn.sigmoid(og_ref[...])).astype(o_ref.dtype)


def _fox_attn_call(qa, ka, kb, vb, proj, batch, seq):
    t = qa.shape[0]
    d = FOX_HEADS * FOX_HD
    tq = min(FOX_TQ, seq)
    nq = seq // tq
    q_map = lambda b, p, i: (b * nq + i, p)
    kv_map = lambda b, p, i: (b, p)
    return pl.pallas_call(
        _fox_attn_kernel,
        grid=(batch, FOX_PAIRS, nq),
        in_specs=[
            pl.BlockSpec((tq, FOX_AUG), q_map),
            pl.BlockSpec((seq, FOX_AUG), kv_map),
            pl.BlockSpec((seq, FOX_AUG), kv_map),
            pl.BlockSpec((seq, LANES), kv_map),
            pl.BlockSpec((tq, LANES), lambda b, p, i: (b * nq + i, 3 * d // LANES + p)),
        ],
        out_specs=pl.BlockSpec((tq, LANES), q_map),
        out_shape=jax.ShapeDtypeStruct((t, d), BF16),
        scratch_shapes=[
            pltpu.VMEM((2, tq), F32),
            pltpu.VMEM((2, tq), F32),
            pltpu.VMEM((LANES, tq), F32),
        ],
        compiler_params=_cparams(("parallel", "parallel", "arbitrary")),
        name="fox_attn",
    )(qa, ka, kb, vb, proj)


def store_token_tiles(ref, val):
    n, width = val.shape
    c = width // LANES
    for j in range(c):
        ref[pl.ds(j, n, stride=c), :] = val[:, j * LANES:(j + 1) * LANES]


def load_token_tiles(ref, n):
    c = ref.shape[0] // n
    return jnp.concatenate([ref[pl.ds(j, n, stride=c), :] for j in range(c)], axis=-1)


def _router_kernel(x_ref, sc_ref, sh_ref, whi_ref, wlo_ref, b_ref, h_ref, idx_ref, wt_ref, cnt_ref):
    h = x_ref[...] * (1.0 + sc_ref[0]) + sh_ref[0]
    store_token_tiles(h_ref, h)
    h_hi = h.astype(BF16)
    h_lo = (h - h_hi.astype(F32)).astype(BF16)
    logits = (jnp.dot(h_hi, whi_ref[...], preferred_element_type=F32)
              + jnp.dot(h_lo, whi_ref[...], preferred_element_type=F32)
              + jnp.dot(h_hi, wlo_ref[...], preferred_element_type=F32)
              + b_ref[...])
    lane = lax.broadcasted_iota(I32, logits.shape, 1)
    logits = jnp.where(lane < N_EXPERTS, logits, NEG_BIG)
    idx_out = jnp.zeros(logits.shape, I32)
    wt_out = jnp.zeros(logits.shape, F32)
    picked = jnp.zeros(logits.shape, I32)
    top0 = None
    denom = None
    for k in range(TOP_K):
        m = jnp.max(logits, axis=-1, keepdims=True)
        sel = jnp.min(jnp.where(logits == m, lane, LANES), axis=-1, keepdims=True)
        hit = lane == sel
        picked = picked + hit.astype(I32)
        logits = jnp.where(hit, NEG_BIG, logits)
        if k == 0:
            top0 = m
        e = jnp.exp(m - top0)
        denom = e if k == 0 else denom + e
        idx_out = jnp.where(lane == k, sel, idx_out)
        wt_out = jnp.where(lane == k, e, wt_out)
    idx_ref[...] = idx_out
    wt_ref[...] = wt_out / denom
    cnt_ref[0] = jnp.sum(picked, axis=0, keepdims=True)


def _router_call(x, sc, sh, w_hi, w_lo, b_pad, seq, tm):
    t, d = x.shape
    per_seq = seq // tm
    const = lambda shape: pl.BlockSpec(shape, lambda i: tuple(0 for _ in shape))
    return pl.pallas_call(
        _router_kernel,
        grid=(t // tm,),
        in_specs=[
            pl.BlockSpec((tm, d), lambda i: (i, 0)),
            pl.BlockSpec((1, 1, d), lambda i: (i // per_seq, 0, 0)),
            pl.BlockSpec((1, 1, d), lambda i: (i // per_seq, 0, 0)),
            const((d, LANES)), const((d, LANES)), const((1, LANES)),
        ],
        out_specs=[
            pl.BlockSpec((tm * (d // LANES), LANES), lambda i: (i, 0)),
            pl.BlockSpec((tm, LANES), lambda i: (i, 0)),
            pl.BlockSpec((tm, LANES), lambda i: (i, 0)),
            pl.BlockSpec((1, 1, LANES), lambda i: (i, 0, 0)),
        ],
        out_shape=[
            jax.ShapeDtypeStruct((t * (d // LANES), LANES), F32),
            jax.ShapeDtypeStruct((t, LANES), I32),
            jax.ShapeDtypeStruct((t, LANES), F32),
            jax.ShapeDtypeStruct((t // tm, 1, LANES), I32),
        ],
        compiler_params=_cparams(("parallel",)),
        name="moe_router",
    )(x, sc, sh, w_hi, w_lo, b_pad)


MOE_GROUP = 2 * LANES


def _moe_kernel(fb_ref, nb_ref, src_ref, dst_ref, nact_ref,
                h_hbm, wgu_ref, bgu_ref, wdn_ref, bdn_ref, perm_ref,
                y_hbm,
                xa, xb, xc, ya, yb, yc, wgu_bf, wdn_bf, gsem, ssem):
    e = pl.program_id(0)
    n_act = nact_ref[0]
    rows = MOE_ROWS
    n_groups = wgu_ref.shape[3] // MOE_GROUP
    tile = xa.shape[0] // rows
    xbufs, ybufs = (xa, xb, xc), (ya, yb, yc)
    n_buf = len(xbufs)

    def gather_start(blk, r, buf):
        tok = src_ref[blk * rows + r]
        pltpu.make_async_copy(h_hbm.at[tok], xbufs[buf].at[pl.ds(r * tile, tile), :], gsem.at[buf]).start()

    def scatter_start(blk_ext, r, buf):
        row = dst_ref[blk_ext * rows + r]
        pltpu.make_async_copy(ybufs[buf].at[pl.ds(r * tile, tile), :], y_hbm.at[row], ssem.at[buf]).start(priority=1)

    def wait_gather(buf):
        pltpu.make_async_copy(xbufs[buf], xbufs[buf], gsem.at[buf]).wait()

    def wait_scatter(buf):
        pltpu.make_async_copy(ybufs[buf], ybufs[buf], ssem.at[buf]).wait()

    def rolled(start_fn, blk, buf):
        def body(r, c):
            start_fn(blk, r, buf)
            return c
        lax.fori_loop(0, rows, body, 0)

    @pl.when(e == 0)
    def _():
        rolled(gather_start, 0, 0)
        rolled(gather_start, 1, 1)
        for buf in range(n_buf):
            ybufs[buf][...] = jnp.zeros_like(ybufs[buf])
        for par in range(2):
            def zero_row(r, c, par=par):
                row = dst_ref[r] + (par - 1) * rows
                pltpu.make_async_copy(ybufs[par].at[pl.ds(r * tile, tile), :], y_hbm.at[row], ssem.at[par]).start()
                return c
            lax.fori_loop(0, rows, zero_row, 0)
            wait_scatter(par)

    def block_step(cur, i):
        prev = (cur + 2) % n_buf
        prev2 = (cur + 1) % n_buf
        wait_gather(cur)
        x = load_token_tiles(xbufs[cur], rows).astype(BF16)
        gu = jnp.dot(x, wgu_bf[...], preferred_element_type=F32) + bgu_ref[0, 0]
        parts = []
        for g in range(n_groups):
            gate = jnp.minimum(gu[:, g * MOE_GROUP:g * MOE_GROUP + LANES], SWIGLU_LIMIT)
            up = jnp.clip(gu[:, g * MOE_GROUP + LANES:(g + 1) * MOE_GROUP], -SWIGLU_LIMIT, SWIGLU_LIMIT)
            glu = gate * jax.nn.sigmoid(SWIGLU_ALPHA * gate)
            parts.append(((up + 1.0) * glu).astype(BF16))
        hmid = jnp.concatenate(parts, axis=-1)
        store_token_tiles(ybufs[cur], jnp.dot(hmid, wdn_bf[...], preferred_element_type=F32) + bdn_ref[0, 0])
        for r in range(rows):
            gather_start(i + 2, r, prev)
        for r in range(rows):
            scatter_start(i, r, prev)

        @pl.when(i >= 1)
        def _():
            wait_scatter(prev2)

        @pl.when(i == n_act - 1)
        def _():
            rolled(scatter_start, i + 1, cur)
            wait_scatter(prev)
            wait_scatter(cur)
            wait_gather(prev2)
            wait_gather(prev)

    perm = perm_ref[...]
    for g in range(n_groups):
        cs = slice(g * MOE_GROUP, (g + 1) * MOE_GROUP)
        wgu_bf[:, cs] = jnp.dot(wgu_ref[0, 0, :, cs].astype(BF16), perm,
                                preferred_element_type=F32).astype(BF16)
    wdn_bf[...] = wdn_ref[0, 0].astype(BF16)

    def expert_block(j, carry):
        i = fb_ref[e] + j
        for buf in range(n_buf):
            @pl.when(i % n_buf == buf)
            def _(buf=buf):
                block_step(buf, i)
        return carry

    lax.fori_loop(0, nb_ref[e], expert_block, 0)


def _moe_call(h, first_blk, n_blk, src, dst, nact, w_gu, b_gu, w_down, b_down, layer, n_out_rows):
    _, n_exp, d, f2 = w_gu.shape
    f = w_down.shape[2]
    perm = np.zeros((MOE_GROUP, MOE_GROUP), np.float32)
    for jj in range(LANES):
        perm[2 * jj, jj] = 1.0
        perm[2 * jj + 1, LANES + jj] = 1.0
    b_gu_grouped = b_gu[layer].reshape(n_exp, f2 // MOE_GROUP, LANES, 2).transpose(0, 1, 3, 2).reshape(
        1, n_exp, 1, f2)
    by_expert = lambda e, *_: (layer, e, 0, 0)
    grid_spec = pltpu.PrefetchScalarGridSpec(
        num_scalar_prefetch=5,
        grid=(n_exp,),
        in_specs=[
            pl.BlockSpec(memory_space=pl.ANY),
            pl.BlockSpec((1, 1, d, f2), by_expert),
            pl.BlockSpec((1, 1, 1, f2), lambda e, *_: (0, e, 0, 0)),
            pl.BlockSpec((1, 1, f, d), by_expert),
            pl.BlockSpec((1, 1, 1, d), by_expert),
            pl.BlockSpec((MOE_GROUP, MOE_GROUP), lambda e, *_: (0, 0)),
        ],
        out_specs=pl.BlockSpec(memory_space=pl.ANY),
        scratch_shapes=[
            *[pltpu.VMEM((MOE_ROWS * (d // LANES), LANES), F32) for _ in range(6)],
            pltpu.VMEM((d, f2), BF16),
            pltpu.VMEM((f, d), BF16),
            pltpu.SemaphoreType.DMA((3,)),
            pltpu.SemaphoreType.DMA((3,)),
        ],
    )
    return pl.pallas_call(
        _moe_kernel,
        grid_spec=grid_spec,
        out_shape=jax.ShapeDtypeStruct((n_out_rows, d // LANES, LANES), F32),
        compiler_params=_cparams(("arbitrary",)),
        name="moe_experts",
    )(first_blk, n_blk, src, dst, nact, h, w_gu, b_gu_grouped, w_down, b_down.reshape(b_down.shape[0], n_exp, 1, d),
      jnp.asarray(perm, BF16))


def _combine_kernel(y0_ref, y1_ref, y2_ref, y3_ref, wt_ref, x_ref, g_ref, lg_ref, lb_ref, o_ref):
    wt = wt_ref[...]
    n = x_ref.shape[0]
    y = load_token_tiles(y0_ref, n) * wt[:, 0:1]
    for k, y_ref in enumerate((y1_ref, y2_ref, y3_ref), start=1):
        y = y + load_token_tiles(y_ref, n) * wt[:, k:k + 1]
    z = ALPHA * x_ref[...] + (1.0 + g_ref[0]) * y
    o_ref[...] = _layer_norm_rows(z, lg_ref[...], lb_ref[...])


def _combine_call(y4, wt, x, gate, ln_g, ln_b, seq, tm):
    t, d = x.shape
    per_seq = seq // tm
    nblk = t // tm
    y_spec = lambda k: pl.BlockSpec((tm * (d // LANES), LANES), lambda i, k=k: (k * nblk + i, 0))
    return pl.pallas_call(
        _combine_kernel,
        grid=(nblk,),
        in_specs=[
            y_spec(0), y_spec(1), y_spec(2), y_spec(3),
            pl.BlockSpec((tm, LANES), lambda i: (i, 0)),
            pl.BlockSpec((tm, d), lambda i: (i, 0)),
            pl.BlockSpec((1, 1, d), lambda i: (i // per_seq, 0, 0)),
            pl.BlockSpec((1, d), lambda i: (0, 0)),
            pl.BlockSpec((1, d), lambda i: (0, 0)),
        ],
        out_specs=pl.BlockSpec((tm, d), lambda i: (i, 0)),
        out_shape=jax.ShapeDtypeStruct((t, d), F32),
        compiler_params=_cparams(("parallel",)),
        name="moe_combine_ln",
    )(y4, y4, y4, y4, wt, x, gate, ln_g.reshape(1, d), ln_b.reshape(1, d))


def _moe_plan(top_idx, block_counts, t):
    n_assign = t * TOP_K
    n_blocks = -(-(n_assign + N_EXPERTS * (MOE_ROWS - 1)) // MOE_ROWS)
    experts = jnp.arange(N_EXPERTS, dtype=I32)
    order = jnp.argsort(top_idx.reshape(-1)).astype(I32)
    counts = jnp.sum(block_counts[:, 0, :N_EXPERTS], axis=0)
    starts = jnp.cumsum(counts) - counts
    padded = (counts + MOE_ROWS - 1) // MOE_ROWS * MOE_ROWS
    pad_ends = jnp.cumsum(padded)
    pad_starts = pad_ends - padded
    blocks = jnp.arange(n_blocks + 2, dtype=I32)
    first_row = blocks * MOE_ROWS
    block_e = jnp.minimum(jnp.sum((pad_ends[None, :] <= first_row[:, None]).astype(I32), axis=1), N_EXPERTS - 1)
    of_block = lambda v: jnp.sum(jnp.where(block_e[:, None] == experts[None, :], v[None, :], 0), axis=1)
    seg_off = first_row - of_block(pad_starts)
    n_valid = jnp.where(first_row < pad_ends[-1], jnp.clip(of_block(counts) - seg_off, 0, MOE_ROWS), 0)
    r = jnp.arange(MOE_ROWS, dtype=I32)
    valid = r[None, :] < n_valid[:, None]
    assign = order[jnp.clip((of_block(starts) + seg_off)[:, None] + r[None, :], 0, n_assign - 1)]
    dump = n_assign + (blocks[:, None] % 2) * MOE_ROWS + r[None, :]
    src = jnp.where(valid, assign // TOP_K, 0).reshape(-1)
    dst = jnp.where(valid, (assign % TOP_K) * t + assign // TOP_K, dump)
    dst_ext = jnp.concatenate([dump[1], dst[:n_blocks].reshape(-1)])
    n_act = (pad_ends[-1] // MOE_ROWS).astype(I32).reshape(1)
    return pad_starts // MOE_ROWS, padded // MOE_ROWS, src, dst_ext, n_act, n_assign + 2 * MOE_ROWS


def _moe_layer(x, sc, sh, gate, ln_g, ln_b, w_router, b_router, w_gu, b_gu, w_down, b_down, layer, seq, tm):
    t, d = x.shape
    w_pad = jnp.zeros((d, LANES), F32).at[:, :N_EXPERTS].set(w_router)
    w_hi = w_pad.astype(BF16)
    w_lo = (w_pad - w_hi.astype(F32)).astype(BF16)
    b_pad = jnp.zeros((1, LANES), F32).at[0, :N_EXPERTS].set(b_router)
    h, idx, wt, block_counts = _router_call(x, sc, sh, w_hi, w_lo, b_pad, seq, tm)
    first_blk, n_blk, src, dst, n_act, n_rows = _moe_plan(idx[:, :TOP_K], block_counts, t)
    y4 = _moe_call(h.reshape(t, d // LANES, LANES), first_blk, n_blk, src, dst, n_act, w_gu, b_gu, w_down, b_down,
                   layer, n_rows)
    return _combine_call(y4.reshape(-1, LANES), wt, x, gate, ln_g, ln_b, seq, tm)


def _pad_cols(w, n):
    return jnp.pad(w, ((0, 0), (0, n - w.shape[1])))


def kernel(x, c, ada_w, ada_b, ln_g, ln_b, gla_w_in, gla_w_gate_up, gla_b_gate, gla_norm_g, gla_w_out, rg_w_in, rg_conv_w, rg_conv_b, rg_w_rg, rg_b_rg, rg_w_ig, rg_b_ig, rg_lambda, rg_w_out, fox_w_in, fox_b_f, fox_q_norm_g, fox_k_norm_g, fox_w_out, moe_w_router, moe_b_router, moe_w_gu, moe_b_gu, moe_w_down, moe_b_down):
    batch, seq, d = x.shape
    t = batch * seq
    tm = min(512, seq)
    depth = ada_w.shape[0]
    xt = x.reshape(t, d)
    mod = _ada_call(c, ada_w, ada_b)

    for l in range(depth):
        part = lambda i: mod[l, :, i * d:(i + 1) * d].reshape(batch, 1, d)
        sh1, sc1, g1, sh2, sc2, g2 = (part(i) for i in range(6))
        kind, j = l % 3, l // 3
        if kind == 0:
            n_in = 2 * GLA_DK + 2 * GLA_DV + LANES
            proj = _modmm_call(xt, sc1, sh1, _pad_cols(gla_w_in[j], n_in).astype(BF16), seq, tm)
            wg = jnp.pad(gla_w_gate_up[j], ((0, LANES - GLA_RANK), (0, 0))).astype(BF16)
            y = _gla_call(proj, wg, gla_b_gate[j], gla_norm_g[j], batch, seq)
            w_out = gla_w_out[j]
        elif kind == 1:
            proj = _modmm_call(xt, sc1, sh1, rg_w_in[j].astype(BF16), seq, tm)
            wgate = jnp.concatenate([rg_w_rg[j], rg_w_ig[j]], axis=-1).astype(BF16)
            y = _rg_call(proj, rg_conv_w[j], rg_conv_b[j], wgate, rg_b_rg[j], rg_b_ig[j],
                         rg_lambda[j], batch, seq)
            w_out = rg_w_out[j]
        else:
            w = fox_w_in[j]
            w = jnp.concatenate([w[:, :3 * d], w[:, 3 * d + FOX_HEADS:], w[:, 3 * d:3 * d + FOX_HEADS]], axis=1)
            proj = _modmm_call(xt, sc1, sh1, _pad_cols(w, 4 * d + LANES).astype(BF16), seq, tm)
            qa, ka, kb, vb = _fox_prep_call(proj, fox_b_f[j], fox_q_norm_g[j], fox_k_norm_g[j], batch, seq)
            y = _fox_attn_call(qa, ka, kb, vb, proj, batch, seq)
            w_out = fox_w_out[j]
        xt = _outln_call(y, w_out.astype(BF16), xt, g1, ln_g[l, 0], ln_b[l, 0], seq, tm)
        xt = _moe_layer(xt, sc2, sh2, g2, ln_g[l, 1], ln_b[l, 1], moe_w_router[l], moe_b_router[l],
                        moe_w_gu, moe_b_gu, moe_w_down, moe_b_down, l, seq, tm)
    return xt.reshape(batch, seq, d)
```

```python
import functools

import numpy as np
import jax
import jax.numpy as jnp
from jax import lax
from jax.experimental import pallas as pl
from jax.experimental.pallas import tpu as pltpu

F32 = jnp.float32
BF16 = jnp.bfloat16
I32 = jnp.int32

DEPTH = 4
ALPHA = (2.0 * DEPTH) ** 0.25
LN_EPS = 1e-5

GLA_HEADS = 4
GLA_HK = 128
GLA_HV = 256
GLA_DK = GLA_HEADS * GLA_HK
GLA_DV = GLA_HEADS * GLA_HV
GLA_RANK = 16
GLA_TAU = 16.0
GLA_CHUNK = 64
GLA_ROWS = 512

RG_BW = 128
RG_BLOCKS = 10
RG_WIDTH = RG_BW * RG_BLOCKS
RG_CONV = 4
RG_C = 8.0
RG_ROWS = 256

FOX_HD = 64
FOX_HEADS = 16
FOX_PAIRS = FOX_HEADS // 2
FOX_TQ = 512
FOX_PREP_ROWS = 256
FOX_AUG = 256

N_EXPERTS = 32
TOP_K = 4
SWIGLU_LIMIT = 7.0
SWIGLU_ALPHA = 1.702
MOE_ROWS = 256
NEG_BIG = -1e30
LOG2E = 1.4426950408889634

LANES = 128
VMEM_LIMIT = 56 * 1024 * 1024


def _cparams(sem, vmem=VMEM_LIMIT):
    return pltpu.CompilerParams(dimension_semantics=sem, vmem_limit_bytes=vmem)


def _softplus(z):
    return jnp.maximum(z, 0.0) + jnp.log(1.0 + jnp.exp(-jnp.abs(z)))


def _layer_norm_rows(z, g, b):
    mu = jnp.mean(z, axis=-1, keepdims=True)
    zc = z - mu
    var = jnp.mean(zc * zc, axis=-1, keepdims=True)
    return zc * lax.rsqrt(var + LN_EPS) * g + b


def _ada_kernel(c_ref, w_ref, b_ref, o_ref):
    c = c_ref[...]
    ca = c * jax.nn.sigmoid(c)
    o_ref[0] = jnp.dot(ca.astype(BF16), w_ref[0].astype(BF16), preferred_element_type=F32) + b_ref[0]


def _ada_call(c, ada_w, ada_b):
    depth, d, n = ada_w.shape
    b = c.shape[0]
    tn = n // 4
    return pl.pallas_call(
        _ada_kernel,
        grid=(depth, n // tn),
        in_specs=[
            pl.BlockSpec((b, d), lambda l, j: (0, 0)),
            pl.BlockSpec((1, d, tn), lambda l, j: (l, 0, j)),
            pl.BlockSpec((1, 1, tn), lambda l, j: (l, 0, j)),
        ],
        out_specs=pl.BlockSpec((1, b, tn), lambda l, j: (l, 0, j)),
        out_shape=jax.ShapeDtypeStruct((depth, b, n), F32),
        compiler_params=_cparams(("parallel", "parallel")),
        name="ada_mod",
    )(c, ada_w, ada_b.reshape(depth, 1, n))


def _modmm_kernel(x_ref, sc_ref, sh_ref, w_ref, o_ref):
    h = x_ref[...] * (1.0 + sc_ref[0]) + sh_ref[0]
    o_ref[...] = jnp.dot(h.astype(BF16), w_ref[...], preferred_element_type=F32)


def _modmm_call(x, sc, sh, w_bf, seq, tm):
    t, d = x.shape
    n = w_bf.shape[1]
    per_seq = seq // tm
    return pl.pallas_call(
        _modmm_kernel,
        grid=(t // tm,),
        in_specs=[
            pl.BlockSpec((tm, d), lambda i: (i, 0)),
            pl.BlockSpec((1, 1, d), lambda i: (i // per_seq, 0, 0)),
            pl.BlockSpec((1, 1, d), lambda i: (i // per_seq, 0, 0)),
            pl.BlockSpec((d, n), lambda i: (0, 0)),
        ],
        out_specs=pl.BlockSpec((tm, n), lambda i: (i, 0)),
        out_shape=jax.ShapeDtypeStruct((t, n), F32),
        compiler_params=_cparams(("parallel",)),
        name="mod_inproj",
    )(x, sc, sh, w_bf)


def _outln_kernel(y_ref, w_ref, x_ref, g_ref, lg_ref, lb_ref, o_ref):
    y = jnp.dot(y_ref[...].astype(BF16), w_ref[...], preferred_element_type=F32)
    z = ALPHA * x_ref[...] + (1.0 + g_ref[0]) * y
    o_ref[...] = _layer_norm_rows(z, lg_ref[...], lb_ref[...])


def _outln_call(y, w_bf, x, gate, ln_g, ln_b, seq, tm):
    t, k = y.shape
    d = x.shape[1]
    per_seq = seq // tm
    return pl.pallas_call(
        _outln_kernel,
        grid=(t // tm,),
        in_specs=[
            pl.BlockSpec((tm, k), lambda i: (i, 0)),
            pl.BlockSpec((k, d), lambda i: (0, 0)),
            pl.BlockSpec((tm, d), lambda i: (i, 0)),
            pl.BlockSpec((1, 1, d), lambda i: (i // per_seq, 0, 0)),
            pl.BlockSpec((1, d), lambda i: (0, 0)),
            pl.BlockSpec((1, d), lambda i: (0, 0)),
        ],
        out_specs=pl.BlockSpec((tm, d), lambda i: (i, 0)),
        out_shape=jax.ShapeDtypeStruct((t, d), F32),
        compiler_params=_cparams(("parallel",)),
        name="outproj_ln",
    )(y, w_bf, x, gate, ln_g.reshape(1, d), ln_b.reshape(1, d))


def _gla_kernel(q_ref, k_ref, v_ref, r_ref, g_ref, wg_ref, bg_ref, ng_ref, tri_ref,
                o_ref, la_ref, st_ref):
    c_rows = GLA_CHUNK
    half = c_rows // 2

    @pl.when(pl.program_id(1) == 0)
    def _():
        st_ref[...] = jnp.zeros_like(st_ref)

    z = jnp.dot(g_ref[...].astype(BF16), wg_ref[...], preferred_element_type=F32) + bg_ref[...]
    la_ref[...] = -_softplus(-z) * (1.0 / GLA_TAU)

    row = lax.broadcasted_iota(I32, (c_rows, c_rows), 0)
    col = lax.broadcasted_iota(I32, (c_rows, c_rows), 1)
    causal = col <= row
    tri = tri_ref[...]
    n_chunks = q_ref.shape[0] // c_rows

    def chunk(c):
        rows = pl.ds(c * c_rows, c_rows)
        bc = jnp.dot(tri, la_ref[rows, :], preferred_element_type=F32, precision=lax.Precision.HIGHEST)
        b_mid = bc[half - 1:half, :]
        b_last = bc[c_rows - 1:c_rows, :]
        e_q_mid = jnp.exp(bc - b_mid)
        e_k_mid = jnp.exp(b_mid - bc)
        e_q = jnp.exp(bc)
        e_k_end = jnp.exp(b_last - bc)
        dec = jnp.exp(b_last)
        qc = q_ref[rows, :] * (GLA_HK ** -0.5)
        kc = k_ref[rows, :]
        for h in range(GLA_HEADS):
            ks = slice(h * GLA_HK, (h + 1) * GLA_HK)
            vs = slice(h * GLA_HV, (h + 1) * GLA_HV)
            qm = (qc[:, ks] * e_q_mid[:, ks]).astype(BF16)
            km = (kc[:, ks] * e_k_mid[:, ks]).astype(BF16)
            s = lax.dot_general(qm, km, (((1,), (1,)), ((), ())), preferred_element_type=F32)
            s = jnp.where(causal, s, 0.0)
            vh = v_ref[rows, vs].astype(BF16)
            o = jnp.dot(s.astype(BF16), vh, preferred_element_type=F32)
            qd = (qc[:, ks] * e_q[:, ks]).astype(BF16)
            st = st_ref[h]
            o = o + lax.dot_general(qd, st.astype(BF16), (((1,), (1,)), ((), ())),
                                    preferred_element_type=F32)
            ke = (kc[:, ks] * e_k_end[:, ks]).astype(BF16)
            st_ref[h] = st * dec[:, ks] + lax.dot_general(
                vh, ke, (((0,), (0,)), ((), ())), preferred_element_type=F32)
            ms = jnp.mean(o * o, axis=-1, keepdims=True)
            y = o * lax.rsqrt(ms + LN_EPS) * ng_ref[:, vs]
            rr = r_ref[rows, vs]
            o_ref[rows, vs] = (y * (rr * jax.nn.sigmoid(rr))).astype(o_ref.dtype)

    for c in range(n_chunks):
        chunk(c)


def _gla_call(proj, wg_pad_bf, b_gate, norm_g, batch, seq):
    t = proj.shape[0]
    rows = min(GLA_ROWS, seq)
    per_seq = seq // rows
    tri = jnp.asarray(np.tril(np.ones((GLA_CHUNK, GLA_CHUNK), np.float32)))
    row_map = lambda b, j: b * per_seq + j
    return pl.pallas_call(
        _gla_kernel,
        grid=(batch, per_seq),
        in_specs=[
            pl.BlockSpec((rows, GLA_DK), lambda b, j: (row_map(b, j), 0)),
            pl.BlockSpec((rows, GLA_DK), lambda b, j: (row_map(b, j), 1)),
            pl.BlockSpec((rows, GLA_DV), lambda b, j: (row_map(b, j), 1)),
            pl.BlockSpec((rows, GLA_DV), lambda b, j: (row_map(b, j), 2)),
            pl.BlockSpec((rows, LANES), lambda b, j: (row_map(b, j), (2 * GLA_DK + 2 * GLA_DV) // LANES)),
            pl.BlockSpec((LANES, GLA_DK), lambda b, j: (0, 0)),
            pl.BlockSpec((1, GLA_DK), lambda b, j: (0, 0)),
            pl.BlockSpec((1, GLA_DV), lambda b, j: (0, 0)),
            pl.BlockSpec((GLA_CHUNK, GLA_CHUNK), lambda b, j: (0, 0)),
        ],
        out_specs=pl.BlockSpec((rows, GLA_DV), lambda b, j: (row_map(b, j), 0)),
        out_shape=jax.ShapeDtypeStruct((t, GLA_DV), BF16),
        scratch_shapes=[
            pltpu.VMEM((rows, GLA_DK), F32),
            pltpu.VMEM((GLA_HEADS, GLA_HV, GLA_HK), F32),
        ],
        compiler_params=_cparams(("parallel", "arbitrary")),
        name="gla_core",
    )(proj, proj, proj, proj, proj, wg_pad_bf, b_gate.reshape(1, GLA_DK), norm_g.reshape(1, GLA_DV), tri)


def _rg_kernel(y_ref, x_ref, cw_ref, cb_ref, wg_ref, brg_ref, big_ref, lam_ref,
               o_ref, xext_ref, hcar_ref):
    rows = x_ref.shape[0]

    @pl.when(pl.program_id(1) == 0)
    def _():
        xext_ref[0:8, :] = jnp.zeros((8, RG_WIDTH), F32)
        hcar_ref[...] = jnp.zeros_like(hcar_ref)

    xext_ref[8:8 + rows, :] = x_ref[...]
    xc = cb_ref[...] + xext_ref[pl.ds(8 - (RG_CONV - 1), rows), :] * cw_ref[0:1, :]
    for j in range(1, RG_CONV):
        xc = xc + xext_ref[pl.ds(8 - (RG_CONV - 1) + j, rows), :] * cw_ref[j:j + 1, :]
    xext_ref[0:8, :] = x_ref[rows - 8:rows, :]

    r_parts, i_parts = [], []
    for n in range(RG_BLOCKS):
        cs = slice(n * RG_BW, (n + 1) * RG_BW)
        g = jnp.dot(xc[:, cs].astype(BF16), wg_ref[n], preferred_element_type=F32)
        r_parts.append(g[:, :RG_BW])
        i_parts.append(g[:, RG_BW:])
    r = jax.nn.sigmoid(jnp.concatenate(r_parts, axis=-1) + brg_ref[...])
    ig = jax.nn.sigmoid(jnp.concatenate(i_parts, axis=-1) + big_ref[...])

    log_a = -RG_C * r * _softplus(-lam_ref[...])
    a = jnp.exp(log_a)
    th = jnp.tanh(log_a)
    u = jnp.sqrt(-2.0 * th / (1.0 - th)) * (ig * xc)

    n_grp = rows // 8
    a3 = a.reshape(n_grp, 8, RG_WIDTH)
    h3 = u.reshape(n_grp, 8, RG_WIDTH)
    sidx = lax.broadcasted_iota(I32, (1, 8, 1), 1)
    for d in (1, 2, 4):
        keep = sidx >= d
        sh_a = pltpu.roll(a3, d, axis=1)
        sh_h = pltpu.roll(h3, d, axis=1)
        h3 = jnp.where(keep, a3 * sh_h + h3, h3)
        a3 = jnp.where(keep, a3 * sh_a, a3)
    carry = hcar_ref[...]
    groups = []
    for g in range(n_grp):
        hg = h3[g] + a3[g] * carry
        carry = hg[7:8, :]
        groups.append(hg)
    h = jnp.concatenate(groups, axis=0)
    hcar_ref[...] = carry

    yb = y_ref[...]
    gelu = 0.5 * yb * (1.0 + jnp.tanh(0.7978845608028654 * (yb + 0.044715 * (yb * yb * yb))))
    o_ref[...] = (h * gelu).astype(o_ref.dtype)


def _rg_call(proj, conv_w, conv_b, wgate_bf, b_rg, b_ig, lam, batch, seq):
    t = proj.shape[0]
    rows = min(RG_ROWS, seq)
    per_seq = seq // rows
    row_map = lambda b, j: b * per_seq + j
    vec = lambda: pl.BlockSpec((1, RG_WIDTH), lambda b, j: (0, 0))
    return pl.pallas_call(
        _rg_kernel,
        grid=(batch, per_seq),
        in_specs=[
            pl.BlockSpec((rows, RG_WIDTH), lambda b, j: (row_map(b, j), 0)),
            pl.BlockSpec((rows, RG_WIDTH), lambda b, j: (row_map(b, j), 1)),
            pl.BlockSpec((RG_CONV, RG_WIDTH), lambda b, j: (0, 0)),
            vec(),
            pl.BlockSpec((RG_BLOCKS, RG_BW, 2 * RG_BW), lambda b, j: (0, 0, 0)),
            vec(), vec(), vec(),
        ],
        out_specs=pl.BlockSpec((rows, RG_WIDTH), lambda b, j: (row_map(b, j), 0)),
        out_shape=jax.ShapeDtypeStruct((t, RG_WIDTH), BF16),
        scratch_shapes=[
            pltpu.VMEM((rows + 8, RG_WIDTH), F32),
            pltpu.VMEM((1, RG_WIDTH), F32),
        ],
        compiler_params=_cparams(("parallel", "arbitrary")),
        name="rglru_core",
    )(proj, proj, conv_w, conv_b.reshape(1, RG_WIDTH), wgate_bf, b_rg.reshape(1, RG_WIDTH),
      b_ig.reshape(1, RG_WIDTH), lam.reshape(1, RG_WIDTH))


def _fox_bias_selectors():
    piece = lambda part, head: part * FOX_HEADS + head
    one = 3 * FOX_HEADS
    sel_q = np.zeros((LANES, FOX_PAIRS * LANES), np.float32)
    sel_ka = np.zeros((LANES, FOX_PAIRS * LANES), np.float32)
    sel_kb = np.zeros((LANES, FOX_PAIRS * LANES), np.float32)
    for p in range(FOX_PAIRS):
        base = p * LANES
        for side, head in enumerate((2 * p, 2 * p + 1)):
            sel_k = sel_ka if side == 0 else sel_kb
            for part in range(3):
                sel_q[piece(part, head), base + 6 * side + part] = 1.0
                sel_q[one, base + 6 * side + 3 + part] = 1.0
                sel_k[one, base + 6 * side + part] = 1.0
                sel_k[piece(part, head), base + 6 * side + 3 + part] = -1.0
    return sel_q, sel_ka, sel_kb


def _fox_prep_kernel(q_ref, k_ref, v_ref, f_ref, bf_ref, gq_ref, gk_ref, grp_ref, tri_ref,
                     sq_ref, ska_ref, skb_ref, qa_ref, ka_ref, kb_ref, vb_ref, dcar_ref):
    @pl.when(pl.program_id(1) == 0)
    def _():
        dcar_ref[...] = jnp.zeros_like(dcar_ref)

    lane = lax.broadcasted_iota(I32, (1, LANES), 1)
    log_f = -_softplus(-(f_ref[...] + bf_ref[...]))
    log_f = jnp.where(lane < FOX_HEADS, log_f, 0.0)
    dcum = jnp.dot(tri_ref[...], log_f, preferred_element_type=F32,
                   precision=lax.Precision.HIGHEST) + dcar_ref[...]
    dcar_ref[...] = dcum[dcum.shape[0] - 1:, :]

    d2 = dcum * LOG2E
    d_hi = d2.astype(BF16)
    rem = d2 - d_hi.astype(F32)
    d_mid = rem.astype(BF16)
    d_lo = (rem - d_mid.astype(F32)).astype(BF16)
    hi, mid, lo = (pltpu.roll(p.astype(F32), s, axis=1) for p, s in
                   ((d_hi, 0), (d_mid, FOX_HEADS), (d_lo, 2 * FOX_HEADS)))
    pieces = jnp.where(lane < FOX_HEADS, hi,
                       jnp.where(lane < 2 * FOX_HEADS, mid,
                                 jnp.where(lane < 3 * FOX_HEADS, lo,
                                           jnp.where(lane == 3 * FOX_HEADS, 1.0, 0.0)))).astype(BF16)
    bias_q = jnp.dot(pieces, sq_ref[...], preferred_element_type=F32).astype(BF16)
    bias_ka = jnp.dot(pieces, ska_ref[...], preferred_element_type=F32).astype(BF16)
    bias_kb = jnp.dot(pieces, skb_ref[...], preferred_element_type=F32).astype(BF16)

    grp = grp_ref[...]
    first_head = lane < FOX_HD

    def rms(xt, gain):
        sq = xt * xt
        sq_hi = sq.astype(BF16)
        sq_lo = (sq - sq_hi.astype(F32)).astype(BF16)
        ssum = (jnp.dot(sq_hi, grp, preferred_element_type=F32)
                + jnp.dot(sq_lo, grp, preferred_element_type=F32))
        return xt * lax.rsqrt(ssum * (1.0 / FOX_HD) + LN_EPS) * gain

    for p in range(FOX_PAIRS):
        src = slice(p * LANES, (p + 1) * LANES)
        dst = slice(p * FOX_AUG, p * FOX_AUG + LANES)
        dst_bias = slice(p * FOX_AUG + LANES, (p + 1) * FOX_AUG)
        qn = rms(q_ref[:, src], gq_ref[...] * (FOX_HD ** -0.5 * LOG2E)).astype(BF16)
        kn = rms(k_ref[:, src], gk_ref[...]).astype(BF16)
        qa_ref[:, dst] = qn
        qa_ref[:, dst_bias] = bias_q[:, src]
        ka_ref[:, dst] = jnp.where(first_head, kn, jnp.zeros_like(kn))
        ka_ref[:, dst_bias] = bias_ka[:, src]
        kb_ref[:, dst] = jnp.where(first_head, jnp.zeros_like(kn), kn)
        kb_ref[:, dst_bias] = bias_kb[:, src]
    vb_ref[...] = v_ref[...].astype(BF16)


def _fox_prep_call(proj, b_f, q_norm_g, k_norm_g, batch, seq):
    t = proj.shape[0]
    d = FOX_HEADS * FOX_HD
    rows = min(FOX_PREP_ROWS, seq)
    per_seq = seq // rows
    row_map = lambda b, j: b * per_seq + j
    sel_q, sel_ka, sel_kb = (jnp.asarray(s, BF16) for s in _fox_bias_selectors())
    grp = np.kron(np.eye(2, dtype=np.float32), np.ones((FOX_HD, FOX_HD), np.float32))
    tri = np.tril(np.ones((rows, rows), np.float32))
    bf_pad = jnp.zeros((1, LANES), F32).at[0, :FOX_HEADS].set(b_f)
    gq2 = jnp.tile(q_norm_g, 2).reshape(1, LANES)
    gk2 = jnp.tile(k_norm_g, 2).reshape(1, LANES)
    const = lambda shape: pl.BlockSpec(shape, lambda b, j: tuple(0 for _ in shape))
    aug = jax.ShapeDtypeStruct((t, FOX_PAIRS * FOX_AUG), BF16)
    aug_spec = pl.BlockSpec((rows, FOX_PAIRS * FOX_AUG), lambda b, j: (row_map(b, j), 0))
    return pl.pallas_call(
        _fox_prep_kernel,
        grid=(batch, per_seq),
        in_specs=[
            pl.BlockSpec((rows, d), lambda b, j: (row_map(b, j), 0)),
            pl.BlockSpec((rows, d), lambda b, j: (row_map(b, j), 1)),
            pl.BlockSpec((rows, d), lambda b, j: (row_map(b, j), 2)),
            pl.BlockSpec((rows, LANES), lambda b, j: (row_map(b, j), 4 * d // LANES)),
            const((1, LANES)), const((1, LANES)), const((1, LANES)),
            const((LANES, LANES)), const((rows, rows)),
            const((LANES, FOX_PAIRS * LANES)), const((LANES, FOX_PAIRS * LANES)),
            const((LANES, FOX_PAIRS * LANES)),
        ],
        out_specs=[aug_spec, aug_spec, aug_spec,
                   pl.BlockSpec((rows, d), lambda b, j: (row_map(b, j), 0))],
        out_shape=[aug, aug, aug, jax.ShapeDtypeStruct((t, d), BF16)],
        scratch_shapes=[pltpu.VMEM((1, LANES), F32)],
        compiler_params=_cparams(("parallel", "arbitrary")),
        name="fox_prep",
    )(proj, proj, proj, proj, bf_pad, gq2, gk2, jnp.asarray(grp, BF16), jnp.asarray(tri),
      sel_q, sel_ka, sel_kb)


def _fox_attn_kernel(qa_ref, ka_ref, kb_ref, v_ref, og_ref, o_ref, m_ref, l_ref, acc_ref):
    qi = pl.program_id(2)
    tq = qa_ref.shape[0]
    sub = lax.broadcasted_iota(I32, (LANES, 1), 0)
    first_head = sub < FOX_HD

    m_ref[...] = jnp.full_like(m_ref, NEG_BIG)
    l_ref[...] = jnp.zeros_like(l_ref)
    acc_ref[...] = jnp.zeros_like(acc_ref)

    def steps(blocks):
        q = qa_ref[...]
        scored = []
        for ki, masked in blocks:
            rows = pl.ds(pl.multiple_of(ki * tq, tq), tq)
            scored.append((rows, masked, [
                lax.dot_general(k_ref[rows, :], q, (((1,), (1,)), ((), ())), preferred_element_type=F32)
                for k_ref in (ka_ref, kb_ref)]))
        for rows, masked, sts in scored:
            v = v_ref[rows, :]
            pv, alpha = [], []
            for side, st in enumerate(sts):
                if masked:
                    key = lax.broadcasted_iota(I32, st.shape, 0)
                    qry = lax.broadcasted_iota(I32, st.shape, 1)
                    st = jnp.where(key <= qry, st, NEG_BIG)
                m_prev = m_ref[side:side + 1, :]
                m_new = jnp.maximum(m_prev, jnp.max(st, axis=0, keepdims=True))
                a = jnp.exp2(m_prev - m_new)
                p = jnp.exp2(st - m_new)
                l_ref[side:side + 1, :] = a * l_ref[side:side + 1, :] + jnp.sum(p, axis=0, keepdims=True)
                m_ref[side:side + 1, :] = m_new
                pv.append(lax.dot_general(v, p.astype(BF16), (((0,), (0,)), ((), ())),
                                          preferred_element_type=F32))
                alpha.append(a)
            acc_ref[...] = (acc_ref[...] * jnp.where(first_head, alpha[0], alpha[1])
                            + jnp.where(first_head, pv[0], pv[1]))

    def body(j, carry):
        steps([(2 * j, False), (2 * j + 1, False)])
        return carry

    lax.fori_loop(0, qi // 2, body, 0)

    @pl.when(qi % 2 == 1)
    def _():
        steps([(qi - 1, False), (qi, True)])

    @pl.when(qi % 2 == 0)
    def _():
        steps([(qi, True)])

    out_t = acc_ref[...] / jnp.where(first_head, l_ref[0:1, :], l_ref[1:2, :])
    o_ref[...] = (out_t.T * jax.nn.sigmoid(og_ref[...])).astype(o_ref.dtype)


def _fox_attn_call(qa, ka, kb, vb, proj, batch, seq):
    t = qa.shape[0]
    d = FOX_HEADS * FOX_HD
    tq = min(FOX_TQ, seq)
    nq = seq // tq
    q_map = lambda b, p, i: (b * nq + i, p)
    kv_map = lambda b, p, i: (b, p)
    return pl.pallas_call(
        _fox_attn_kernel,
        grid=(batch, FOX_PAIRS, nq),
        in_specs=[
            pl.BlockSpec((tq, FOX_AUG), q_map),
            pl.BlockSpec((seq, FOX_AUG), kv_map),
            pl.BlockSpec((seq, FOX_AUG), kv_map),
            pl.BlockSpec((seq, LANES), kv_map),
            pl.BlockSpec((tq, LANES), lambda b, p, i: (b * nq + i, 3 * d // LANES + p)),
        ],
        out_specs=pl.BlockSpec((tq, LANES), q_map),
        out_shape=jax.ShapeDtypeStruct((t, d), BF16),
        scratch_shapes=[
            pltpu.VMEM((2, tq), F32),
            pltpu.VMEM((2, tq), F32),
            pltpu.VMEM((LANES, tq), F32),
        ],
        compiler_params=_cparams(("parallel", "parallel", "arbitrary")),
        name="fox_attn",
    )(qa, ka, kb, vb, proj)


def store_token_tiles(ref, val):
    n, width = val.shape
    c = width // LANES
    for j in range(c):
        ref[pl.ds(j, n, stride=c), :] = val[:, j * LANES:(j + 1) * LANES]


def load_token_tiles(ref, n):
    c = ref.shape[0] // n
    return jnp.concatenate([ref[pl.ds(j, n, stride=c), :] for j in range(c)], axis=-1)


def _router_kernel(x_ref, sc_ref, sh_ref, whi_ref, wlo_ref, b_ref, h_ref, idx_ref, wt_ref, cnt_ref):
    h = x_ref[...] * (1.0 + sc_ref[0]) + sh_ref[0]
    store_token_tiles(h_ref, h)
    h_hi = h.astype(BF16)
    h_lo = (h - h_hi.astype(F32)).astype(BF16)
    logits = (jnp.dot(h_hi, whi_ref[...], preferred_element_type=F32)
              + jnp.dot(h_lo, whi_ref[...], preferred_element_type=F32)
              + jnp.dot(h_hi, wlo_ref[...], preferred_element_type=F32)
              + b_ref[...])
    tm = logits.shape[0]
    lt = logits.T[:N_EXPERTS]
    row = lax.broadcasted_iota(I32, lt.shape, 0)
    picked = jnp.zeros(lt.shape, F32)
    sels, exps = [], []
    top0 = None
    denom = None
    for k in range(TOP_K):
        m = jnp.max(lt, axis=0, keepdims=True)
        sel = jnp.min(jnp.where(lt == m, row, N_EXPERTS), axis=0, keepdims=True)
        hit = row == sel
        picked = picked + hit.astype(F32)
        lt = jnp.where(hit, NEG_BIG, lt)
        if k == 0:
            top0 = m
        e = jnp.exp(m - top0)
        denom = e if k == 0 else denom + e
        sels.append(sel.astype(F32))
        exps.append(e)
    packed = jnp.concatenate(sels + [e / denom for e in exps] + [jnp.zeros((LANES - 2 * TOP_K, tm), F32)], axis=0)
    res = packed.T
    idx_ref[...] = res.astype(I32)
    wt_ref[...] = res
    picked_rows = jnp.concatenate([picked, jnp.zeros((LANES - N_EXPERTS, tm), F32)], axis=0).astype(BF16)
    cnt = lax.dot_general(jnp.ones((8, tm), BF16), picked_rows, (((1,), (1,)), ((), ())),
                          preferred_element_type=F32)
    cnt_ref[0] = cnt[0:1].astype(I32)


def _router_call(x, sc, sh, w_hi, w_lo, b_pad, seq, tm):
    t, d = x.shape
    per_seq = seq // tm
    const = lambda shape: pl.BlockSpec(shape, lambda i: tuple(0 for _ in shape))
    return pl.pallas_call(
        _router_kernel,
        grid=(t // tm,),
        in_specs=[
            pl.BlockSpec((tm, d), lambda i: (i, 0)),
            pl.BlockSpec((1, 1, d), lambda i: (i // per_seq, 0, 0)),
            pl.BlockSpec((1, 1, d), lambda i: (i // per_seq, 0, 0)),
            const((d, LANES)), const((d, LANES)), const((1, LANES)),
        ],
        out_specs=[
            pl.BlockSpec((tm * (d // LANES), LANES), lambda i: (i, 0)),
            pl.BlockSpec((tm, LANES), lambda i: (i, 0)),
            pl.BlockSpec((tm, LANES), lambda i: (i, 0)),
            pl.BlockSpec((1, 1, LANES), lambda i: (i, 0, 0)),
        ],
        out_shape=[
            jax.ShapeDtypeStruct((t * (d // LANES), LANES), F32),
            jax.ShapeDtypeStruct((t, LANES), I32),
            jax.ShapeDtypeStruct((t, LANES), F32),
            jax.ShapeDtypeStruct((t // tm, 1, LANES), I32),
        ],
        compiler_params=_cparams(("parallel",)),
        name="moe_router",
    )(x, sc, sh, w_hi, w_lo, b_pad)


MOE_GROUP = 2 * LANES


def _moe_kernel(fb_ref, nb_ref, src_ref, dst_ref, nact_ref,
                h_hbm, wgu_ref, bgu_ref, wdn_ref, bdn_ref, perm_ref,
                y_hbm,
                xa, xb, xc, ya, yb, yc, wgu_bf, wdn_bf, gsem, ssem):
    e = pl.program_id(0)
    n_act = nact_ref[0]
    rows = MOE_ROWS
    n_groups = wgu_ref.shape[3] // MOE_GROUP
    tile = xa.shape[0] // rows
    xbufs, ybufs = (xa, xb, xc), (ya, yb, yc)
    n_buf = len(xbufs)

    def gather_start(blk, r, buf):
        tok = src_ref[blk * rows + r]
        pltpu.make_async_copy(h_hbm.at[tok], xbufs[buf].at[pl.ds(r * tile, tile), :], gsem.at[buf]).start()

    def scatter_start(blk_ext, r, buf):
        row = dst_ref[blk_ext * rows + r]
        pltpu.make_async_copy(ybufs[buf].at[pl.ds(r * tile, tile), :], y_hbm.at[row], ssem.at[buf]).start(priority=1)

    def wait_gather(buf):
        pltpu.make_async_copy(xbufs[buf], xbufs[buf], gsem.at[buf]).wait()

    def wait_scatter(buf):
        pltpu.make_async_copy(ybufs[buf], ybufs[buf], ssem.at[buf]).wait()

    def rolled(start_fn, blk, buf):
        def body(r, c):
            start_fn(blk, r, buf)
            return c
        lax.fori_loop(0, rows, body, 0)

    @pl.when(e == 0)
    def _():
        rolled(gather_start, 0, 0)
        rolled(gather_start, 1, 1)
        for buf in range(n_buf):
            ybufs[buf][...] = jnp.zeros_like(ybufs[buf])
        for par in range(2):
            def zero_row(r, c, par=par):
                row = dst_ref[r] + (par - 1) * rows
                pltpu.make_async_copy(ybufs[par].at[pl.ds(r * tile, tile), :], y_hbm.at[row], ssem.at[par]).start()
                return c
            lax.fori_loop(0, rows, zero_row, 0)
            wait_scatter(par)

    def block_step(cur, i):
        prev = (cur + 2) % n_buf
        prev2 = (cur + 1) % n_buf
        wait_gather(cur)
        x = load_token_tiles(xbufs[cur], rows).astype(BF16)
        gu = jnp.dot(x, wgu_bf[...], preferred_element_type=F32) + bgu_ref[0, 0]
        parts = []
        for g in range(n_groups):
            gate = jnp.minimum(gu[:, g * MOE_GROUP:g * MOE_GROUP + LANES], SWIGLU_LIMIT)
            up = jnp.clip(gu[:, g * MOE_GROUP + LANES:(g + 1) * MOE_GROUP], -SWIGLU_LIMIT, SWIGLU_LIMIT)
            glu = gate * jax.nn.sigmoid(SWIGLU_ALPHA * gate)
            parts.append(((up + 1.0) * glu).astype(BF16))
        hmid = jnp.concatenate(parts, axis=-1)
        store_token_tiles(ybufs[cur], jnp.dot(hmid, wdn_bf[...], preferred_element_type=F32) + bdn_ref[0, 0])
        for r in range(rows):
            gather_start(i + 2, r, prev)
        for r in range(rows):
            scatter_start(i, r, prev)

        @pl.when(i >= 1)
        def _():
            wait_scatter(prev2)

        @pl.when(i == n_act - 1)
        def _():
            rolled(scatter_start, i + 1, cur)
            wait_scatter(prev)
            wait_scatter(cur)
            wait_gather(prev2)
            wait_gather(prev)

    perm = perm_ref[...]
    for g in range(n_groups):
        cs = slice(g * MOE_GROUP, (g + 1) * MOE_GROUP)
        wgu_bf[:, cs] = jnp.dot(wgu_ref[0, 0, :, cs].astype(BF16), perm,
                                preferred_element_type=F32).astype(BF16)
    wdn_bf[...] = wdn_ref[0, 0].astype(BF16)

    def expert_block(j, carry):
        i = fb_ref[e] + j
        for buf in range(n_buf):
            @pl.when(i % n_buf == buf)
            def _(buf=buf):
                block_step(buf, i)
        return carry

    lax.fori_loop(0, nb_ref[e], expert_block, 0)


def _moe_call(h, first_blk, n_blk, src, dst, nact, w_gu, b_gu, w_down, b_down, layer, n_out_rows):
    _, n_exp, d, f2 = w_gu.shape
    f = w_down.shape[2]
    perm = np.zeros((MOE_GROUP, MOE_GROUP), np.float32)
    for jj in range(LANES):
        perm[2 * jj, jj] = 1.0
        perm[2 * jj + 1, LANES + jj] = 1.0
    b_gu_grouped = b_gu[layer].reshape(n_exp, f2 // MOE_GROUP, LANES, 2).transpose(0, 1, 3, 2).reshape(
        1, n_exp, 1, f2)
    by_expert = lambda e, *_: (layer, e, 0, 0)
    grid_spec = pltpu.PrefetchScalarGridSpec(
        num_scalar_prefetch=5,
        grid=(n_exp,),
        in_specs=[
            pl.BlockSpec(memory_space=pl.ANY),
            pl.BlockSpec((1, 1, d, f2), by_expert),
            pl.BlockSpec((1, 1, 1, f2), lambda e, *_: (0, e, 0, 0)),
            pl.BlockSpec((1, 1, f, d), by_expert),
            pl.BlockSpec((1, 1, 1, d), by_expert),
            pl.BlockSpec((MOE_GROUP, MOE_GROUP), lambda e, *_: (0, 0)),
        ],
        out_specs=pl.BlockSpec(memory_space=pl.ANY),
        scratch_shapes=[
            *[pltpu.VMEM((MOE_ROWS * (d // LANES), LANES), F32) for _ in range(6)],
            pltpu.VMEM((d, f2), BF16),
            pltpu.VMEM((f, d), BF16),
            pltpu.SemaphoreType.DMA((3,)),
            pltpu.SemaphoreType.DMA((3,)),
        ],
    )
    return pl.pallas_call(
        _moe_kernel,
        grid_spec=grid_spec,
        out_shape=jax.ShapeDtypeStruct((n_out_rows, d // LANES, LANES), F32),
        compiler_params=_cparams(("arbitrary",)),
        name="moe_experts",
    )(first_blk, n_blk, src, dst, nact, h, w_gu, b_gu_grouped, w_down, b_down.reshape(b_down.shape[0], n_exp, 1, d),
      jnp.asarray(perm, BF16))


def _combine_kernel(y0_ref, y1_ref, y2_ref, y3_ref, wt_ref, x_ref, g_ref, lg_ref, lb_ref, o_ref):
    wt = wt_ref[...]
    n = x_ref.shape[0]
    y = load_token_tiles(y0_ref, n) * wt[:, TOP_K:TOP_K + 1]
    for k, y_ref in enumerate((y1_ref, y2_ref, y3_ref), start=1):
        y = y + load_token_tiles(y_ref, n) * wt[:, TOP_K + k:TOP_K + k + 1]
    z = ALPHA * x_ref[...] + (1.0 + g_ref[0]) * y
    o_ref[...] = _layer_norm_rows(z, lg_ref[...], lb_ref[...])


def _combine_call(y4, wt, x, gate, ln_g, ln_b, seq, tm):
    t, d = x.shape
    per_seq = seq // tm
    nblk = t // tm
    y_spec = lambda k: pl.BlockSpec((tm * (d // LANES), LANES), lambda i, k=k: (k * nblk + i, 0))
    return pl.pallas_call(
        _combine_kernel,
        grid=(nblk,),
        in_specs=[
            y_spec(0), y_spec(1), y_spec(2), y_spec(3),
            pl.BlockSpec((tm, LANES), lambda i: (i, 0)),
            pl.BlockSpec((tm, d), lambda i: (i, 0)),
            pl.BlockSpec((1, 1, d), lambda i: (i // per_seq, 0, 0)),
            pl.BlockSpec((1, d), lambda i: (0, 0)),
            pl.BlockSpec((1, d), lambda i: (0, 0)),
        ],
        out_specs=pl.BlockSpec((tm, d), lambda i: (i, 0)),
        out_shape=jax.ShapeDtypeStruct((t, d), F32),
        compiler_params=_cparams(("parallel",)),
        name="moe_combine_ln",
    )(y4, y4, y4, y4, wt, x, gate, ln_g.reshape(1, d), ln_b.reshape(1, d))


def _moe_plan(top_idx, block_counts, t):
    n_assign = t * TOP_K
    n_blocks = -(-(n_assign + N_EXPERTS * (MOE_ROWS - 1)) // MOE_ROWS)
    experts = jnp.arange(N_EXPERTS, dtype=I32)
    order = jnp.argsort(top_idx.reshape(-1)).astype(I32)
    counts = jnp.sum(block_counts[:, 0, :N_EXPERTS], axis=0)
    starts = jnp.cumsum(counts) - counts
    padded = (counts + MOE_ROWS - 1) // MOE_ROWS * MOE_ROWS
    pad_ends = jnp.cumsum(padded)
    pad_starts = pad_ends - padded
    blocks = jnp.arange(n_blocks + 2, dtype=I32)
    first_row = blocks * MOE_ROWS
    block_e = jnp.minimum(jnp.sum((pad_ends[None, :] <= first_row[:, None]).astype(I32), axis=1), N_EXPERTS - 1)
    of_block = lambda v: jnp.sum(jnp.where(block_e[:, None] == experts[None, :], v[None, :], 0), axis=1)
    seg_off = first_row - of_block(pad_starts)
    n_valid = jnp.where(first_row < pad_ends[-1], jnp.clip(of_block(counts) - seg_off, 0, MOE_ROWS), 0)
    r = jnp.arange(MOE_ROWS, dtype=I32)
    valid = r[None, :] < n_valid[:, None]
    assign = order[jnp.clip((of_block(starts) + seg_off)[:, None] + r[None, :], 0, n_assign - 1)]
    dump = n_assign + (blocks[:, None] % 2) * MOE_ROWS + r[None, :]
    src = jnp.where(valid, assign // TOP_K, 0).reshape(-1)
    dst = jnp.where(valid, (assign % TOP_K) * t + assign // TOP_K, dump)
    dst_ext = jnp.concatenate([dump[1], dst[:n_blocks].reshape(-1)])
    n_act = (pad_ends[-1] // MOE_ROWS).astype(I32).reshape(1)
    return pad_starts // MOE_ROWS, padded // MOE_ROWS, src, dst_ext, n_act, n_assign + 2 * MOE_ROWS


def _moe_layer(x, sc, sh, gate, ln_g, ln_b, w_router, b_router, w_gu, b_gu, w_down, b_down, layer, seq, tm):
    t, d = x.shape
    w_pad = jnp.zeros((d, LANES), F32).at[:, :N_EXPERTS].set(w_router)
    w_hi = w_pad.astype(BF16)
    w_lo = (w_pad - w_hi.astype(F32)).astype(BF16)
    b_pad = jnp.zeros((1, LANES), F32).at[0, :N_EXPERTS].set(b_router)
    h, idx, wt, block_counts = _router_call(x, sc, sh, w_hi, w_lo, b_pad, seq, tm)
    first_blk, n_blk, src, dst, n_act, n_rows = _moe_plan(idx[:, :TOP_K], block_counts, t)
    y4 = _moe_call(h.reshape(t, d // LANES, LANES), first_blk, n_blk, src, dst, n_act, w_gu, b_gu, w_down, b_down,
                   layer, n_rows)
    return _combine_call(y4.reshape(-1, LANES), wt, x, gate, ln_g, ln_b, seq, tm)


def _pad_cols(w, n):
    return jnp.pad(w, ((0, 0), (0, n - w.shape[1])))


def kernel(x, c, ada_w, ada_b, ln_g, ln_b, gla_w_in, gla_w_gate_up, gla_b_gate, gla_norm_g, gla_w_out, rg_w_in, rg_conv_w, rg_conv_b, rg_w_rg, rg_b_rg, rg_w_ig, rg_b_ig, rg_lambda, rg_w_out, fox_w_in, fox_b_f, fox_q_norm_g, fox_k_norm_g, fox_w_out, moe_w_router, moe_b_router, moe_w_gu, moe_b_gu, moe_w_down, moe_b_down):
    batch, seq, d = x.shape
    t = batch * seq
    tm = min(512, seq)
    depth = ada_w.shape[0]
    xt = x.reshape(t, d)
    mod = _ada_call(c, ada_w, ada_b)

    for l in range(depth):
        part = lambda i: mod[l, :, i * d:(i + 1) * d].reshape(batch, 1, d)
        sh1, sc1, g1, sh2, sc2, g2 = (part(i) for i in range(6))
        kind, j = l % 3, l // 3
        if kind == 0:
            n_in = 2 * GLA_DK + 2 * GLA_DV + LANES
            proj = _modmm_call(xt, sc1, sh1, _pad_cols(gla_w_in[j], n_in).astype(BF16), seq, tm)
            wg = jnp.pad(gla_w_gate_up[j], ((0, LANES - GLA_RANK), (0, 0))).astype(BF16)
            y = _gla_call(proj, wg, gla_b_gate[j], gla_norm_g[j], batch, seq)
            w_out = gla_w_out[j]
        elif kind == 1:
            proj = _modmm_call(xt, sc1, sh1, rg_w_in[j].astype(BF16), seq, tm)
            wgate = jnp.concatenate([rg_w_rg[j], rg_w_ig[j]], axis=-1).astype(BF16)
            y = _rg_call(proj, rg_conv_w[j], rg_conv_b[j], wgate, rg_b_rg[j], rg_b_ig[j],
                         rg_lambda[j], batch, seq)
            w_out = rg_w_out[j]
        else:
            w = fox_w_in[j]
            w = jnp.concatenate([w[:, :3 * d], w[:, 3 * d + FOX_HEADS:], w[:, 3 * d:3 * d + FOX_HEADS]], axis=1)
            proj = _modmm_call(xt, sc1, sh1, _pad_cols(w, 4 * d + LANES).astype(BF16), seq, tm)
            qa, ka, kb, vb = _fox_prep_call(proj, fox_b_f[j], fox_q_norm_g[j], fox_k_norm_g[j], batch, seq)
            y = _fox_attn_call(qa, ka, kb, vb, proj, batch, seq)
            w_out = fox_w_out[j]
        xt = _outln_call(y, w_out.astype(BF16), xt, g1, ln_g[l, 0], ln_b[l, 0], seq, tm)
        xt = _moe_layer(xt, sc2, sh2, g2, ln_g[l, 1], ln_b[l, 1], moe_w_router[l], moe_b_router[l],
                        moe_w_gu, moe_b_gu, moe_w_down, moe_b_down, l, seq, tm)
    return xt.reshape(batch, seq, d)
```
